```python
import jax, jax.numpy as jnp
from jax import lax
import numpy as np

D_MODEL = 2048
BATCH = 4
SEQ = 4096
DEPTH = 2

CHUNK = 64
D_FF = 5632
SSM_EXPAND = 2
D_INNER = SSM_EXPAND * D_MODEL
SSM_HEAD_DIM = 64
SSM_HEADS = D_INNER // SSM_HEAD_DIM
SSM_GROUPS = 8
SSM_STATE = 128
SSM_CONV = 4
SSM_CONV_DIM = D_INNER + 2 * SSM_GROUPS * SSM_STATE
CONF_CHANNELS = D_MODEL
CONF_KERNEL = 31
N_BRANCHES = 2
IN_SPLIT_SIZES = (D_INNER, SSM_CONV_DIM, SSM_HEADS, 2 * CONF_CHANNELS, N_BRANCHES * D_MODEL)
IN_COLS = sum(IN_SPLIT_SIZES)
IN_SPLIT_POINTS = tuple(int(v) for v in np.cumsum(IN_SPLIT_SIZES)[:-1])
N_ADA = 9
EPS = 1e-6

kernel_name = "hybrid_ssd_conformer_macaron_adaln"


def rmsnorm(x, g):
    x32 = x.astype(jnp.float32)
    y = x32 * lax.rsqrt(jnp.mean(x32 * x32, axis=-1, keepdims=True) + EPS)
    return (y * g.astype(jnp.float32)).astype(x.dtype)


def layernorm(x, g, b):
    x32 = x.astype(jnp.float32)
    mu = jnp.mean(x32, axis=-1, keepdims=True)
    var = jnp.mean(jnp.square(x32 - mu), axis=-1, keepdims=True)
    y = (x32 - mu) * lax.rsqrt(var + EPS)
    return (y * g.astype(jnp.float32) + b.astype(jnp.float32)).astype(x.dtype)


def modulate(h, shift, scale):
    return h * (1.0 + scale) + shift


def causal_dwconv(u, w, b):
    k = w.shape[0]
    out = lax.conv_general_dilated(
        u, w[:, None, :].astype(u.dtype), window_strides=(1,), padding=[(k - 1, 0)],
        dimension_numbers=("NWC", "WIO", "NWC"), feature_group_count=u.shape[-1])
    return out + b


def swiglu(h, w13, w2):
    a, g = jnp.split(h @ w13, 2, axis=-1)
    return (jax.nn.silu(g) * a) @ w2


def ssd_chunked(xs, dt, a_log, b_ssm, c_ssm):
    f32 = jnp.float32
    bsz, seq = xs.shape[:2]
    n_chunks = seq // CHUNK
    hg = SSM_HEADS // SSM_GROUPS
    x = xs.astype(f32).reshape(bsz, n_chunks, CHUNK, SSM_GROUPS, hg, SSM_HEAD_DIM)
    dtc = dt.reshape(bsz, n_chunks, CHUNK, SSM_GROUPS, hg)
    bc = b_ssm.astype(f32).reshape(bsz, n_chunks, CHUNK, SSM_GROUPS, SSM_STATE)
    cc = c_ssm.astype(f32).reshape(bsz, n_chunks, CHUNK, SSM_GROUPS, SSM_STATE)
    a = -jnp.exp(a_log.astype(f32)).reshape(SSM_GROUPS, hg)
    a_cs = jnp.cumsum(dtc * a, axis=2)
    x_dt = x * dtc[..., None]
    causal = jnp.tril(jnp.ones((CHUNK, CHUNK), bool))[None, None, :, :, None, None]
    seg = a_cs[:, :, :, None] - a_cs[:, :, None, :]
    decay = jnp.exp(jnp.where(causal, seg, -jnp.inf))
    cb = jnp.einsum("bclgn,bcsgn->bclsg", cc, bc)
    y_diag = jnp.einsum("bclsg,bclsgh,bcsghp->bclghp", cb, decay, x_dt)
    to_end = jnp.exp(a_cs[:, :, -1:] - a_cs)
    states = jnp.einsum("bclgn,bclgh,bclghp->bcghpn", bc, to_end, x_dt)
    chunk_decay = jnp.exp(a_cs[:, :, -1])

    def carry_state(h, inp):
        s, d = inp
        return h * d[..., None, None] + s, h

    h0 = jnp.zeros((bsz, SSM_GROUPS, hg, SSM_HEAD_DIM, SSM_STATE), f32)
    _, prev = lax.scan(carry_state, h0, (jnp.moveaxis(states, 1, 0), jnp.moveaxis(chunk_decay, 1, 0)))
    prev = jnp.moveaxis(prev, 0, 1)
    y_off = jnp.einsum("bclgn,bcghpn,bclgh->bclghp", cc, prev, jnp.exp(a_cs))
    return (y_diag + y_off).reshape(bsz, seq, SSM_HEADS, SSM_HEAD_DIM)


def hybrid_mixer(h, w_in, ssm_conv_w, ssm_conv_b, dt_bias, a_log, d_skip, ssm_norm_w,
                 w_ssm_out, dw_w, dw_b, conv_ln_g, conv_ln_b, w_pw2, b_pw2, w_o):
    bsz, seq, _ = h.shape
    proj = h @ w_in
    z, xbc, dt_raw, glu_in, gate_in = jnp.split(proj, IN_SPLIT_POINTS, axis=-1)
    xbc = jax.nn.silu(causal_dwconv(xbc, ssm_conv_w, ssm_conv_b))
    xs, b_ssm, c_ssm = jnp.split(xbc, (D_INNER, D_INNER + SSM_GROUPS * SSM_STATE), axis=-1)
    xs = xs.reshape(bsz, seq, SSM_HEADS, SSM_HEAD_DIM)
    b_ssm = b_ssm.reshape(bsz, seq, SSM_GROUPS, SSM_STATE)
    c_ssm = c_ssm.reshape(bsz, seq, SSM_GROUPS, SSM_STATE)
    dt = jax.nn.softplus((dt_raw + dt_bias).astype(jnp.float32))
    y = ssd_chunked(xs, dt, a_log, b_ssm, c_ssm)
    y = y + d_skip.astype(jnp.float32)[:, None] * xs.astype(jnp.float32)
    y = y.reshape(bsz, seq, D_INNER).astype(h.dtype)
    y = rmsnorm(y * jax.nn.silu(z), ssm_norm_w)
    y_ssm = y @ w_ssm_out
    u_a, u_g = jnp.split(glu_in, 2, axis=-1)
    u = u_a * jax.nn.sigmoid(u_g)
    u = causal_dwconv(u, dw_w, dw_b)
    u = jax.nn.silu(layernorm(u, conv_ln_g, conv_ln_b))
    y_conv = u @ w_pw2 + b_pw2
    g_ssm, g_conv = jnp.split(jax.nn.sigmoid(gate_in), N_BRANCHES, axis=-1)
    return (g_ssm * y_ssm + g_conv * y_conv) @ w_o


def setup_inputs(seed: int = 0) -> dict:
    key = jax.random.key(seed)
    ks = jax.random.split(key, 32)
    f32 = jnp.float32

    def nrm(k, shape, scale):
        return jax.random.normal(k, shape, f32) * scale

    def gain(k, shape):
        return 1.0 + nrm(k, shape, 0.02)

    dt0 = jnp.exp(jax.random.uniform(ks[10], (DEPTH, SSM_HEADS), f32) * (np.log(0.1) - np.log(0.001)) + np.log(0.001))
    return {
        "x": nrm(ks[0], (BATCH, SEQ, D_MODEL), 1.0),
        "c": nrm(ks[1], (BATCH, D_MODEL), 1.0),
        "w_ada": nrm(ks[2], (DEPTH, D_MODEL, N_ADA * D_MODEL), 0.5 * D_MODEL ** -0.5),
        "b_ada": nrm(ks[3], (DEPTH, N_ADA * D_MODEL), 0.02),
        "norm_ffn1": gain(ks[4], (DEPTH, D_MODEL)),
        "ffn1_w13": nrm(ks[5], (DEPTH, D_MODEL, 2 * D_FF), D_MODEL ** -0.5),
        "ffn1_w2": nrm(ks[6], (DEPTH, D_FF, D_MODEL), D_FF ** -0.5),
        "norm_mix": gain(ks[7], (DEPTH, D_MODEL)),
        "w_in": nrm(ks[8], (DEPTH, D_MODEL, IN_COLS), D_MODEL ** -0.5),
        "ssm_conv_w": nrm(ks[9], (DEPTH, SSM_CONV, SSM_CONV_DIM), SSM_CONV ** -0.5),
        "ssm_conv_b": nrm(ks[11], (DEPTH, SSM_CONV_DIM), 0.02),
        "dt_bias": dt0 + jnp.log(-jnp.expm1(-dt0)),
        "a_log": jnp.log(jax.random.uniform(ks[12], (DEPTH, SSM_HEADS), f32, 1.0, 16.0)),
        "d_skip": 1.0 + nrm(ks[13], (DEPTH, SSM_HEADS), 0.1),
        "ssm_norm_w": gain(ks[14], (DEPTH, D_INNER)),
        "w_ssm_out": nrm(ks[15], (DEPTH, D_INNER, D_MODEL), D_INNER ** -0.5),
        "dw_w": nrm(ks[16], (DEPTH, CONF_KERNEL, CONF_CHANNELS), CONF_KERNEL ** -0.5),
        "dw_b": nrm(ks[17], (DEPTH, CONF_CHANNELS), 0.02),
        "conv_ln_g": gain(ks[18], (DEPTH, CONF_CHANNELS)),
        "conv_ln_b": nrm(ks[19], (DEPTH, CONF_CHANNELS), 0.02),
        "w_pw2": nrm(ks[20], (DEPTH, CONF_CHANNELS, D_MODEL), CONF_CHANNELS ** -0.5),
        "b_pw2": nrm(ks[21], (DEPTH, D_MODEL), 0.02),
        "w_o": nrm(ks[22], (DEPTH, D_MODEL, D_MODEL), D_MODEL ** -0.5),
        "norm_ffn2": gain(ks[23], (DEPTH, D_MODEL)),
        "ffn2_w13": nrm(ks[24], (DEPTH, D_MODEL, 2 * D_FF), D_MODEL ** -0.5),
        "ffn2_w2": nrm(ks[25], (DEPTH, D_FF, D_MODEL), D_FF ** -0.5),
        "final_norm": gain(ks[26], (D_MODEL,)),
    }


def reference(x, c, w_ada, b_ada, norm_ffn1, ffn1_w13, ffn1_w2, norm_mix, w_in,
              ssm_conv_w, ssm_conv_b, dt_bias, a_log, d_skip, ssm_norm_w, w_ssm_out,
              dw_w, dw_b, conv_ln_g, conv_ln_b, w_pw2, b_pw2, w_o,
              norm_ffn2, ffn2_w13, ffn2_w2, final_norm):
    c_act = jax.nn.silu(c)
    for l in range(DEPTH):
        ada = (c_act @ w_ada[l] + b_ada[l])[:, None, :]
        sh1, sc1, g1, sh2, sc2, g2, sh3, sc3, g3 = jnp.split(ada, N_ADA, axis=-1)
        h = modulate(rmsnorm(x, norm_ffn1[l]), sh1, sc1)
        x = x + 0.5 * g1 * swiglu(h, ffn1_w13[l], ffn1_w2[l])
        h = modulate(rmsnorm(x, norm_mix[l]), sh2, sc2)
        x = x + g2 * hybrid_mixer(h, w_in[l], ssm_conv_w[l], ssm_conv_b[l], dt_bias[l], a_log[l],
                                  d_skip[l], ssm_norm_w[l], w_ssm_out[l], dw_w[l], dw_b[l],
                                  conv_ln_g[l], conv_ln_b[l], w_pw2[l], b_pw2[l], w_o[l])
        h = modulate(rmsnorm(x, norm_ffn2[l]), sh3, sc3)
        x = x + 0.5 * g3 * swiglu(h, ffn2_w13[l], ffn2_w2[l])
    return rmsnorm(x, final_norm)
```

```python
import functools

import jax
import jax.numpy as jnp
from jax import lax
from jax.experimental import pallas as pl
from jax.experimental.pallas import tpu as pltpu

F32 = jnp.float32
BF16 = jnp.bfloat16
EPS = 1e-6

SSM_GROUPS = 8
N_ADA = 9

LANES = 128
SUBLANES = 8
VMEM_LIMIT_BYTES = 60 * 1024 * 1024

SSD_CHUNK = 128
HEAD_DIM = 64


def _params(semantics):
    return pltpu.CompilerParams(dimension_semantics=semantics,
                                vmem_limit_bytes=VMEM_LIMIT_BYTES)


def _pick(total, want):
    t = min(total, want)
    while total % t:
        t //= 2
    return t


def _silu(v):
    return v * jax.nn.sigmoid(v)


def _softplus(v):
    return jnp.maximum(v, 0.0) + jnp.log1p(jnp.exp(-jnp.abs(v)))


def _rms_mod(x, nw, shift, scale):
    y = x * lax.rsqrt(jnp.mean(x * x, axis=-1, keepdims=True) + EPS) * nw
    return y * (1.0 + scale) + shift


def _dot(a, b):
    return jnp.dot(a, b, preferred_element_type=F32)


def _ada_body(c_ref, w_ref, b_ref, o_ref):
    c = c_ref[...]
    o_ref[0] = _dot(_silu(c).astype(BF16), w_ref[0].astype(BF16)) + b_ref[0]


def _ada(c, w_ada, b_ada):
    depth, d, n = w_ada.shape
    bsz = c.shape[0]
    rows = -(-bsz // SUBLANES) * SUBLANES
    c_pad = jnp.pad(c, ((0, rows - bsz), (0, 0)))
    tn = _pick(n, 1024)
    out = pl.pallas_call(
        _ada_body,
        grid=(depth, n // tn),
        in_specs=[
            pl.BlockSpec((rows, d), lambda l, j: (0, 0)),
            pl.BlockSpec((1, d, tn), lambda l, j: (l, 0, j)),
            pl.BlockSpec((1, 1, tn), lambda l, j: (l, 0, j)),
        ],
        out_specs=pl.BlockSpec((1, rows, tn), lambda l, j: (l, 0, j)),
        out_shape=jax.ShapeDtypeStruct((depth, rows, n), F32),
        compiler_params=_params(("parallel", "parallel")),
        name="ada",
    )(c_pad, w_ada, b_ada.reshape(depth, 1, n))
    return out[:, :bsz].reshape(depth, bsz, N_ADA, d)


def _ffn_body(x_ref, ada_ref, nw_ref, w1_ref, w3_ref, w2_ref, fw_ref, o_ref, h_ref,
              *, ada_base, final):
    j = pl.program_id(1)

    @pl.when(j == 0)
    def _():
        shift = ada_ref[0, ada_base:ada_base + 1, :]
        scale = ada_ref[0, ada_base + 1:ada_base + 2, :]
        h_ref[...] = _rms_mod(x_ref[...], nw_ref[...], shift, scale).astype(BF16)

    h = h_ref[...]
    a = _dot(h, w1_ref[...])
    g = _dot(h, w3_ref[...])
    u = (_silu(g) * a).astype(BF16)
    contrib = _dot(u, w2_ref[...])

    @pl.when(j == 0)
    def _():
        o_ref[...] = contrib

    @pl.when(j > 0)
    def _():
        o_ref[...] += contrib

    @pl.when(j == pl.num_programs(1) - 1)
    def _():
        gate = ada_ref[0, ada_base + 2:ada_base + 3, :]
        y = x_ref[...] + 0.5 * gate * o_ref[...]
        if final:
            y = y * lax.rsqrt(jnp.mean(y * y, axis=-1, keepdims=True) + EPS) * fw_ref[...]
        o_ref[...] = y


def _ffn(x2, ada, ada_base, nw, w13, w2, fw, seq, final):
    m, d = x2.shape
    f = w2.shape[0]
    tm = _pick(seq, 1024)
    tf = _pick(f, 512)
    nf = f // tf
    per_batch = seq // tm
    return pl.pallas_call(
        functools.partial(_ffn_body, ada_base=ada_base, final=final),
        grid=(m // tm, nf),
        in_specs=[
            pl.BlockSpec((tm, d), lambda i, j: (i, 0), pipeline_mode=pl.Buffered(1)),
            pl.BlockSpec((1, N_ADA, d), lambda i, j: (i // per_batch, 0, 0)),
            pl.BlockSpec((1, d), lambda i, j: (0, 0)),
            pl.BlockSpec((d, tf), lambda i, j: (0, j)),
            pl.BlockSpec((d, tf), lambda i, j: (0, j + nf)),
            pl.BlockSpec((tf, d), lambda i, j: (j, 0)),
            pl.BlockSpec((1, d), lambda i, j: (0, 0)),
        ],
        out_specs=pl.BlockSpec((tm, d), lambda i, j: (i, 0)),
        out_shape=jax.ShapeDtypeStruct((m, d), F32),
        scratch_shapes=[pltpu.VMEM((tm, d), BF16)],
        compiler_params=_params(("parallel", "arbitrary")),
        name="ffn",
    )(x2, ada, nw.reshape(1, d), w13, w13, w2, fw.reshape(1, d))


def _inproj_body(x_ref, ada_ref, nw_ref, *rest, ada_base, mode):
    h_ref = rest[-1]
    o_ref = rest[-2]

    @pl.when(pl.program_id(1) == 0)
    def _():
        shift = ada_ref[0, ada_base:ada_base + 1, :]
        scale = ada_ref[0, ada_base + 1:ada_base + 2, :]
        h_ref[...] = _rms_mod(x_ref[...], nw_ref[...], shift, scale).astype(BF16)

    h = h_ref[...]
    if mode == "plain":
        o_ref[...] = _dot(h, rest[0][...])
    elif mode == "sigmoid":
        o_ref[...] = jax.nn.sigmoid(_dot(h, rest[0][...]))
    elif mode == "softplus_bias":
        o_ref[...] = _softplus(_dot(h, rest[0][...]) + rest[1][...])
    elif mode == "glu":
        o_ref[...] = _dot(h, rest[0][...]) * jax.nn.sigmoid(_dot(h, rest[1][...]))
    else:
        raise ValueError(mode)


def _inproj(x2, ada, ada_base, nw, ws, bias, seq, mode, tn_want):
    m, d = x2.shape
    n = ws[0].shape[1]
    tm = _pick(seq, 1024)
    tn = _pick(n, tn_want)
    per_batch = seq // tm
    in_specs = [
        pl.BlockSpec((tm, d), lambda i, j: (i, 0)),
        pl.BlockSpec((1, N_ADA, d), lambda i, j: (i // per_batch, 0, 0)),
        pl.BlockSpec((1, d), lambda i, j: (0, 0)),
    ]
    args = [x2, ada, nw.reshape(1, d)]
    for w in ws:
        in_specs.append(pl.BlockSpec((d, tn), lambda i, j: (0, j)))
        args.append(w)
    if bias is not None:
        in_specs.append(pl.BlockSpec((1, tn), lambda i, j: (0, j)))
        args.append(bias)
    return pl.pallas_call(
        functools.partial(_inproj_body, ada_base=ada_base, mode=mode),
        grid=(m // tm, n // tn),
        in_specs=in_specs,
        out_specs=pl.BlockSpec((tm, tn), lambda i, j: (i, j)),
        out_shape=jax.ShapeDtypeStruct((m, n), F32),
        scratch_shapes=[pltpu.VMEM((tm, d), BF16)],
        compiler_params=_params(("parallel", "arbitrary")),
        name="inproj_" + mode,
    )(*args)


def _halo_rows(taps):
    return -(-(taps - 1) // SUBLANES) * SUBLANES


def _load_tile_with_halo(u_ref, ext_ref, first, halo, rows):
    @pl.when(first)
    def _():
        ext_ref[0:halo, :] = jnp.zeros((halo, ext_ref.shape[1]), F32)

    @pl.when(jnp.logical_not(first))
    def _():
        ext_ref[0:halo, :] = ext_ref[rows:rows + halo, :]

    ext_ref[halo:halo + rows, :] = u_ref[0]


def _conv_block(ext_ref, w_ref, b_ref, taps, halo, r0, rb, cols):
    base = halo - (taps - 1) + r0
    acc = b_ref[:, cols] + w_ref[0:1, cols] * ext_ref[base:base + rb, cols]
    for k in range(1, taps):
        acc = acc + w_ref[k:k + 1, cols] * ext_ref[base + k:base + k + rb, cols]
    return acc


def _ssm_conv_body(u_ref, w_ref, b_ref, o_ref, ext_ref, *, taps, rb, cb):
    rows, width = u_ref.shape[1], u_ref.shape[2]
    halo = _halo_rows(taps)
    _load_tile_with_halo(u_ref, ext_ref, pl.program_id(2) == 0, halo, rows)
    for r0 in range(0, rows, rb):
        for c0 in range(0, width, cb):
            cols = slice(c0, c0 + cb)
            acc = _conv_block(ext_ref, w_ref, b_ref, taps, halo, r0, rb, cols)
            o_ref[0, r0:r0 + rb, cols] = _silu(acc)


def _ssm_conv(zx3, col_off, conv_w, conv_b):
    bsz, seq, _ = zx3.shape
    taps, c = conv_w.shape
    tc = _pick(c, 512)
    tr = _pick(seq, 512)
    assert col_off % tc == 0
    off = col_off // tc
    halo = _halo_rows(taps)
    return pl.pallas_call(
        functools.partial(_ssm_conv_body, taps=taps, rb=_pick(tr, 64), cb=_pick(tc, 256)),
        grid=(bsz, c // tc, seq // tr),
        in_specs=[
            pl.BlockSpec((1, tr, tc), lambda b, j, t: (b, t, off + j)),
            pl.BlockSpec((taps, tc), lambda b, j, t: (0, j)),
            pl.BlockSpec((1, tc), lambda b, j, t: (0, j)),
        ],
        out_specs=pl.BlockSpec((1, tr, tc), lambda b, j, t: (b, t, j)),
        out_shape=jax.ShapeDtypeStruct((bsz, seq, c), F32),
        scratch_shapes=[pltpu.VMEM((tr + halo, tc), F32)],
        compiler_params=_params(("parallel", "parallel", "arbitrary")),
        name="ssm_conv",
    )(zx3, conv_w, conv_b.reshape(1, c))


def _conf_conv_body(u_ref, w_ref, b_ref, g_ref, beta_ref, o_ref, ext_ref, y_ref, *, taps, rb, cb):
    rows, width = u_ref.shape[1], u_ref.shape[2]
    halo = _halo_rows(taps)
    _load_tile_with_halo(u_ref, ext_ref, pl.program_id(1) == 0, halo, rows)

    def lane_block(ci, carry):
        cols = pl.ds(pl.multiple_of(ci * cb, cb), cb)
        for r0 in range(0, rows, rb):
            y_ref[r0:r0 + rb, cols] = _conv_block(ext_ref, w_ref, b_ref, taps, halo, r0, rb, cols)
        return carry

    lax.fori_loop(0, width // cb, lane_block, 0)
    y = y_ref[...]
    mu = jnp.mean(y, axis=-1, keepdims=True)
    yc = y - mu
    var = jnp.mean(yc * yc, axis=-1, keepdims=True)
    yn = yc * lax.rsqrt(var + EPS) * g_ref[...] + beta_ref[...]
    o_ref[0] = _silu(yn).astype(BF16)


def _conf_conv(u3, dw_w, dw_b, ln_g, ln_b):
    bsz, seq, c = u3.shape
    taps = dw_w.shape[0]
    tr = _pick(seq, 256)
    halo = _halo_rows(taps)
    vec = lambda v: v.reshape(1, c)
    row_spec = pl.BlockSpec((1, c), lambda b, t: (0, 0))
    return pl.pallas_call(
        functools.partial(_conf_conv_body, taps=taps, rb=_pick(tr, 64), cb=_pick(c, 256)),
        grid=(bsz, seq // tr),
        in_specs=[
            pl.BlockSpec((1, tr, c), lambda b, t: (b, t, 0)),
            pl.BlockSpec((taps, c), lambda b, t: (0, 0)),
            row_spec, row_spec, row_spec,
        ],
        out_specs=pl.BlockSpec((1, tr, c), lambda b, t: (b, t, 0)),
        out_shape=jax.ShapeDtypeStruct((bsz, seq, c), BF16),
        scratch_shapes=[pltpu.VMEM((tr + halo, c), F32), pltpu.VMEM((tr, c), F32)],
        compiler_params=_params(("parallel", "arbitrary")),
        name="conf_conv",
    )(u3, dw_w, vec(dw_b), vec(ln_g), vec(ln_b))


def _split3(v):
    hi = v.astype(BF16)
    r1 = v - hi.astype(F32)
    mid = r1.astype(BF16)
    lo = (r1 - mid.astype(F32)).astype(BF16)
    return hi, mid, lo


def _ssd_body(xs_ref, b_ref, c_ref, z_ref, dt_ref, alog_ref, dsk_ref, nw_ref,
              yg_ref, ss_ref, s_ref):
    t = pl.program_id(1)
    g = pl.program_id(2)
    q = SSD_CHUNK
    rows = xs_ref.shape[1]
    gw = xs_ref.shape[2]
    n_pairs = gw // LANES

    @pl.when(t == 0)
    def _():
        s_ref[g] = jnp.zeros(s_ref.shape[1:], F32)

    @pl.when(g == 0)
    def _():
        ss_ref[...] = jnp.zeros(ss_ref.shape, F32)

    a_row = -jnp.exp(alog_ref[0])
    causal = lax.broadcasted_iota(jnp.int32, (q, q), 0) >= lax.broadcasted_iota(jnp.int32, (q, q), 1)
    tri = causal.astype(BF16)
    lo_half = lax.broadcasted_iota(jnp.int32, (q, LANES), 1) < HEAD_DIM

    for r0 in range(0, rows, q):
        rs = slice(r0, r0 + q)
        dt = dt_ref[0, rs, :]
        hi, mid, lo = _split3(dt * a_row)
        cs = _dot(tri, jnp.concatenate([hi, mid, lo], axis=1))
        acs = (cs[:, :LANES] + cs[:, LANES:2 * LANES]) + cs[:, 2 * LANES:]
        acs_t = acs.T
        bg = b_ref[0, rs, :].astype(BF16)
        cg = c_ref[0, rs, :].astype(BF16)
        cb = lax.dot_general(cg, bg, (((1,), (1,)), ((), ())), preferred_element_type=F32)
        state = s_ref[g]
        y_off = _dot(cg, state.astype(BF16))

        xw_parts = []
        decay_parts = []
        ss_acc = jnp.zeros((q, 1), F32)
        for p in range(n_pairs):
            ls = slice(p * LANES, (p + 1) * LANES)
            h0, h1 = 2 * p, 2 * p + 1
            bc0 = jnp.broadcast_to(acs[:, h0:h0 + 1], (q, LANES))
            bc1 = jnp.broadcast_to(acs[:, h1:h1 + 1], (q, LANES))
            dt0 = jnp.broadcast_to(dt[:, h0:h0 + 1], (q, LANES))
            dt1 = jnp.broadcast_to(dt[:, h1:h1 + 1], (q, LANES))
            m0 = cb * jnp.exp(jnp.where(causal, bc0 - acs_t[h0:h0 + 1, :], -jnp.inf))
            m1 = cb * jnp.exp(jnp.where(causal, bc1 - acs_t[h1:h1 + 1, :], -jnp.inf))
            acs_pair = jnp.where(lo_half, bc0, bc1)
            dt_pair = jnp.where(lo_half, dt0, dt1)
            xp = xs_ref[0, rs, ls]
            xdt = xp * dt_pair
            lhs = jnp.concatenate([m0.astype(BF16), m1.astype(BF16)], axis=1)
            rhs = jnp.concatenate([jnp.where(lo_half, xdt, 0.0).astype(BF16),
                                   jnp.where(lo_half, 0.0, xdt).astype(BF16)], axis=0)
            y = _dot(lhs, rhs) + y_off[:, ls] * jnp.exp(acs_pair) + dsk_ref[0, :, ls] * xp
            last_pair = acs_pair[q - 1:q, :]
            xw_parts.append((xdt * jnp.exp(last_pair - acs_pair)).astype(BF16))
            decay_parts.append(jnp.exp(last_pair))
            v = y * _silu(z_ref[0, rs, ls])
            ss_acc = ss_acc + jnp.sum(v * v, axis=-1, keepdims=True)
            yg_ref[0, rs, ls] = (v * nw_ref[:, ls]).astype(BF16)

        xw = jnp.concatenate(xw_parts, axis=1)
        new = lax.dot_general(bg, xw, (((0,), (0,)), ((), ())), preferred_element_type=F32)
        s_ref[g] = state * jnp.concatenate(decay_parts, axis=1) + new
        ss_ref[0, rs, :] += jnp.broadcast_to(ss_acc, (q, LANES))


def _ssd(xbc3, zx3, dt3, a_pad, dsk, nw, d_inner):
    bsz, seq, _ = xbc3.shape
    groups = SSM_GROUPS
    gw = d_inner // groups
    n_state = (xbc3.shape[2] - d_inner) // (2 * groups)
    assert gw % LANES == 0 and d_inner % n_state == 0
    tr = _pick(seq, 256)
    assert tr % SSD_CHUNK == 0
    b_off = d_inner // n_state
    return pl.pallas_call(
        _ssd_body,
        grid=(bsz, seq // tr, groups),
        in_specs=[
            pl.BlockSpec((1, tr, gw), lambda b, t, g: (b, t, g)),
            pl.BlockSpec((1, tr, n_state), lambda b, t, g: (b, t, b_off + g)),
            pl.BlockSpec((1, tr, n_state), lambda b, t, g: (b, t, b_off + groups + g)),
            pl.BlockSpec((1, tr, gw), lambda b, t, g: (b, t, g)),
            pl.BlockSpec((1, tr, LANES), lambda b, t, g: (b, t, g)),
            pl.BlockSpec((1, 1, LANES), lambda b, t, g: (g, 0, 0)),
            pl.BlockSpec((1, 1, gw), lambda b, t, g: (g, 0, 0)),
            pl.BlockSpec((1, gw), lambda b, t, g: (0, g)),
        ],
        out_specs=[
            pl.BlockSpec((1, tr, gw), lambda b, t, g: (b, t, g)),
            pl.BlockSpec((1, tr, LANES), lambda b, t, g: (b, t, 0)),
        ],
        out_shape=[
            jax.ShapeDtypeStruct((bsz, seq, d_inner), BF16),
            jax.ShapeDtypeStruct((bsz, seq, LANES), F32),
        ],
        scratch_shapes=[pltpu.VMEM((groups, n_state, gw), F32)],
        compiler_params=_params(("parallel", "arbitrary", "arbitrary")),
        name="ssd",
    )(xbc3, xbc3, xbc3, zx3, dt3, a_pad, dsk, nw.reshape(1, d_inner))


def _merge_body(yg_ref, ss_ref, uc_ref, gs_ref, gc_ref, x_ref, ada_ref, wso_ref, wpw_ref,
                bpw_ref, wo_ref, o_ref, *, ada_base, d_inner):
    j = pl.program_id(1)
    row_scale = lax.rsqrt(ss_ref[:, 0:1] * (1.0 / d_inner) + EPS)
    y_ssm = _dot(yg_ref[...], wso_ref[...]) * row_scale
    y_conv = _dot(uc_ref[...], wpw_ref[...]) + bpw_ref[...]
    merged = (gs_ref[...] * y_ssm + gc_ref[...] * y_conv).astype(BF16)
    contrib = _dot(merged, wo_ref[...])

    @pl.when(j == 0)
    def _():
        o_ref[...] = contrib

    @pl.when(j > 0)
    def _():
        o_ref[...] += contrib

    @pl.when(j == pl.num_programs(1) - 1)
    def _():
        o_ref[...] = x_ref[...] + ada_ref[0, ada_base + 2:ada_base + 3, :] * o_ref[...]


def _merge(yg2, ss2, uc2, gates2, x2, ada, ada_base, wso, wpw, bpw, wo, seq):
    m, d = x2.shape
    d_inner = yg2.shape[1]
    c = uc2.shape[1]
    tm = _pick(seq, 512)
    tn = _pick(d, 512)
    nn = d // tn
    per_batch = seq // tm
    return pl.pallas_call(
        functools.partial(_merge_body, ada_base=ada_base, d_inner=d_inner),
        grid=(m // tm, nn),
        in_specs=[
            pl.BlockSpec((tm, d_inner), lambda i, j: (i, 0)),
            pl.BlockSpec((tm, LANES), lambda i, j: (i, 0)),
            pl.BlockSpec((tm, c), lambda i, j: (i, 0)),
            pl.BlockSpec((tm, tn), lambda i, j: (i, j)),
            pl.BlockSpec((tm, tn), lambda i, j: (i, j + nn)),
            pl.BlockSpec((tm, d), lambda i, j: (i, 0)),
            pl.BlockSpec((1, N_ADA, d), lambda i, j: (i // per_batch, 0, 0)),
            pl.BlockSpec((d_inner, tn), lambda i, j: (0, j)),
            pl.BlockSpec((c, tn), lambda i, j: (0, j)),
            pl.BlockSpec((1, tn), lambda i, j: (0, j)),
            pl.BlockSpec((tn, d), lambda i, j: (j, 0)),
        ],
        out_specs=pl.BlockSpec((tm, d), lambda i, j: (i, 0)),
        out_shape=jax.ShapeDtypeStruct((m, d), F32),
        compiler_params=_params(("parallel", "arbitrary")),
        name="merge",
    )(yg2, ss2, uc2, gates2, gates2, x2, ada, wso, wpw, bpw.reshape(1, d), wo)


def _pad_heads_to_lanes(v, groups):
    lead = v.shape[:-1]
    hg = v.shape[-1] // groups
    v = v.reshape(lead + (groups, hg))
    v = jnp.pad(v, [(0, 0)] * (len(lead) + 1) + [(0, LANES - hg)])
    return v.reshape(lead + (groups * LANES,))


def kernel(x, c, w_ada, b_ada, norm_ffn1, ffn1_w13, ffn1_w2, norm_mix, w_in, ssm_conv_w, ssm_conv_b, dt_bias, a_log, d_skip, ssm_norm_w, w_ssm_out, dw_w, dw_b, conv_ln_g, conv_ln_b, w_pw2, b_pw2, w_o, norm_ffn2, ffn2_w13, ffn2_w2, final_norm):
    bsz, seq, d = x.shape
    depth = w_ada.shape[0]
    groups = SSM_GROUPS
    d_inner = w_ssm_out.shape[1]
    conv_dim = ssm_conv_w.shape[2]
    heads = dt_bias.shape[1]
    conf_c = dw_w.shape[2]
    assert d_inner // heads == HEAD_DIM
    p_dt = d_inner + conv_dim
    p_glu = p_dt + heads
    p_gate = p_glu + 2 * conf_c

    ada_all = _ada(c, w_ada, b_ada)
    x2 = x.reshape(bsz * seq, d)
    for l in range(depth):
        ada = ada_all[l]
        wl = w_in[l]
        w_zx = wl[:, :p_dt].astype(BF16)
        w_dt = _pad_heads_to_lanes(wl[:, p_dt:p_glu], groups).astype(BF16)
        w_glu_a = wl[:, p_glu:p_glu + conf_c].astype(BF16)
        w_glu_g = wl[:, p_glu + conf_c:p_gate].astype(BF16)
        w_gate = wl[:, p_gate:].astype(BF16)
        dt_b = _pad_heads_to_lanes(dt_bias[l], groups).reshape(1, groups * LANES)
        a_pad = _pad_heads_to_lanes(a_log[l], groups).reshape(groups, 1, LANES)
        dsk = jnp.repeat(d_skip[l], HEAD_DIM).reshape(groups, 1, d_inner // groups)

        x2 = _ffn(x2, ada, 0, norm_ffn1[l], ffn1_w13[l].astype(BF16), ffn1_w2[l].astype(BF16),
                  final_norm, seq, False)

        zx = _inproj(x2, ada, 3, norm_mix[l], [w_zx], None, seq, "plain", 1024)
        dt = _inproj(x2, ada, 3, norm_mix[l], [w_dt], dt_b, seq, "softplus_bias", 1024)
        u = _inproj(x2, ada, 3, norm_mix[l], [w_glu_a, w_glu_g], None, seq, "glu", 512)
        gates = _inproj(x2, ada, 3, norm_mix[l], [w_gate], None, seq, "sigmoid", 1024)

        zx3 = zx.reshape(bsz, seq, p_dt)
        xbc = _ssm_conv(zx3, d_inner, ssm_conv_w[l], ssm_conv_b[l])
        yg, ss = _ssd(xbc, zx3, dt.reshape(bsz, seq, groups * LANES), a_pad, dsk,
                      ssm_norm_w[l], d_inner)
        uc = _conf_conv(u.reshape(bsz, seq, conf_c), dw_w[l], dw_b[l], conv_ln_g[l], conv_ln_b[l])

        x2 = _merge(yg.reshape(bsz * seq, d_inner), ss.reshape(bsz * seq, LANES),
                    uc.reshape(bsz * seq, conf_c), gates, x2, ada, 3,
                    w_ssm_out[l].astype(BF16), w_pw2[l].astype(BF16), b_pw2[l],
                    w_o[l].astype(BF16), seq)

        x2 = _ffn(x2, ada, 6, norm_ffn2[l], ffn2_w13[l].astype(BF16), ffn2_w2[l].astype(BF16),
                  final_norm, seq, l == depth - 1)
    return x2.reshape(bsz, seq, d)
```

```python
import functools
import math

import jax
import jax.numpy as jnp
from jax import lax
from jax.experimental import pallas as pl
from jax.experimental.pallas import tpu as pltpu

F32 = jnp.float32
BF16 = jnp.bfloat16
EPS = 1e-6
LOG2E = 1.4426950408889634

SSM_GROUPS = 8
N_ADA = 9

LANES = 128
SUBLANES = 8
VMEM_LIMIT_BYTES = 60 * 1024 * 1024

SSD_CHUNK = 128
HEAD_DIM = 64


def _params(semantics):
    return pltpu.CompilerParams(dimension_semantics=semantics,
                                vmem_limit_bytes=VMEM_LIMIT_BYTES)


def _pick(total, want):
    t = min(total, want)
    while total % t:
        t //= 2
    return t


def _silu(v):
    return v * jax.nn.sigmoid(v)


def _softplus(v):
    return jnp.maximum(v, 0.0) + jnp.log1p(jnp.exp(-jnp.abs(v)))


def _rms_mod(x, nw, shift, scale):
    y = x * lax.rsqrt(jnp.mean(x * x, axis=-1, keepdims=True) + EPS) * nw
    return y * (1.0 + scale) + shift


def _dot(a, b):
    return jnp.dot(a, b, preferred_element_type=F32)


def _ada_body(c_ref, w_ref, b_ref, o_ref):
    c = c_ref[...]
    o_ref[0] = _dot(_silu(c).astype(BF16), w_ref[0].astype(BF16)) + b_ref[0]


def _ada(c, w_ada, b_ada):
    depth, d, n = w_ada.shape
    bsz = c.shape[0]
    rows = -(-bsz // SUBLANES) * SUBLANES
    c_pad = jnp.pad(c, ((0, rows - bsz), (0, 0)))
    tn = _pick(n, 1024)
    out = pl.pallas_call(
        _ada_body,
        grid=(depth, n // tn),
        in_specs=[
            pl.BlockSpec((rows, d), lambda l, j: (0, 0)),
            pl.BlockSpec((1, d, tn), lambda l, j: (l, 0, j)),
            pl.BlockSpec((1, 1, tn), lambda l, j: (l, 0, j)),
        ],
        out_specs=pl.BlockSpec((1, rows, tn), lambda l, j: (l, 0, j)),
        out_shape=jax.ShapeDtypeStruct((depth, rows, n), F32),
        compiler_params=_params(("parallel", "parallel")),
        name="ada",
    )(c_pad, w_ada, b_ada.reshape(depth, 1, n))
    return out[:, :bsz].reshape(depth, bsz, N_ADA, d)


def _ffn_body(x_ref, ada_ref, nw_ref, w1_ref, w3_ref, w2_ref, fw_ref, o_ref, h_ref,
              *, ada_base, final):
    j = pl.program_id(1)

    @pl.when(j == 0)
    def _():
        shift = ada_ref[0, ada_base:ada_base + 1, :]
        scale = ada_ref[0, ada_base + 1:ada_base + 2, :]
        h_ref[...] = _rms_mod(x_ref[...], nw_ref[...], shift, scale).astype(BF16)
        o_ref[...] = jnp.zeros(o_ref.shape, F32)

    h = h_ref[...]
    a = _dot(h, w1_ref[...])
    g = _dot(h, w3_ref[...])
    u = (_silu(g) * a).astype(BF16)
    o_ref[...] += _dot(u, w2_ref[...])

    @pl.when(j == pl.num_programs(1) - 1)
    def _():
        gate = ada_ref[0, ada_base + 2:ada_base + 3, :]
        y = x_ref[...] + 0.5 * gate * o_ref[...]
        if final:
            y = y * lax.rsqrt(jnp.mean(y * y, axis=-1, keepdims=True) + EPS) * fw_ref[...]
        o_ref[...] = y


def _ffn(x2, ada, ada_base, nw, w13, w2, fw, seq, final):
    m, d = x2.shape
    f = w2.shape[0]
    tm = _pick(seq, 1024)
    tf = _pick(f, 512)
    nf = f // tf
    per_batch = seq // tm
    return pl.pallas_call(
        functools.partial(_ffn_body, ada_base=ada_base, final=final),
        grid=(m // tm, nf),
        in_specs=[
            pl.BlockSpec((tm, d), lambda i, j: (i, 0), pipeline_mode=pl.Buffered(1)),
            pl.BlockSpec((1, N_ADA, d), lambda i, j: (i // per_batch, 0, 0)),
            pl.BlockSpec((1, d), lambda i, j: (0, 0)),
            pl.BlockSpec((d, tf), lambda i, j: (0, j)),
            pl.BlockSpec((d, tf), lambda i, j: (0, j + nf)),
            pl.BlockSpec((tf, d), lambda i, j: (j, 0)),
            pl.BlockSpec((1, d), lambda i, j: (0, 0)),
        ],
        out_specs=pl.BlockSpec((tm, d), lambda i, j: (i, 0)),
        out_shape=jax.ShapeDtypeStruct((m, d), F32),
        scratch_shapes=[pltpu.VMEM((tm, d), BF16)],
        compiler_params=_params(("parallel", "arbitrary")),
        name="ffn",
    )(x2, ada, nw.reshape(1, d), w13, w13, w2, fw.reshape(1, d))


def _inproj_body(x_ref, ada_ref, nw_ref, w_ref, wdt_ref, dtb_ref,
                 zx_ref, gate_ref, u_ref, dt_ref, h_ref, *, ada_base, n_zx, n_gate, n_glu):
    j = pl.program_id(1)
    n_main = n_zx + n_gate + n_glu

    @pl.when(j == 0)
    def _():
        shift = ada_ref[0, ada_base:ada_base + 1, :]
        scale = ada_ref[0, ada_base + 1:ada_base + 2, :]
        h_ref[...] = _rms_mod(x_ref[...], nw_ref[...], shift, scale).astype(BF16)

    @pl.when(j < n_zx)
    def _():
        zx_ref[...] = _dot(h_ref[...], w_ref[...])

    @pl.when(jnp.logical_and(j >= n_zx, j < n_zx + n_gate))
    def _():
        gate_ref[...] = jax.nn.sigmoid(_dot(h_ref[...], w_ref[...]))

    @pl.when(jnp.logical_and(j >= n_zx + n_gate, j < n_main))
    def _():
        acc = _dot(h_ref[...], w_ref[...])
        half = acc.shape[1] // 2
        u_ref[...] = acc[:, :half] * jax.nn.sigmoid(acc[:, half:])

    @pl.when(j == n_main)
    def _():
        dt_ref[...] = _softplus(_dot(h_ref[...], wdt_ref[...]) + dtb_ref[...])


def _inproj(x2, ada, ada_base, nw, w_main, w_dt, dt_b, widths, seq, tn):
    m, d = x2.shape
    zx_w, gate_w, glu_w = widths
    n_zx, n_gate, n_glu = zx_w // tn, gate_w // tn, 2 * glu_w // tn
    n_main = n_zx + n_gate + n_glu
    tm = _pick(seq, 1024)
    per_batch = seq // tm
    dtw = w_dt.shape[1]
    return pl.pallas_call(
        functools.partial(_inproj_body, ada_base=ada_base, n_zx=n_zx, n_gate=n_gate, n_glu=n_glu),
        grid=(m // tm, n_main + 1),
        in_specs=[
            pl.BlockSpec((tm, d), lambda i, j: (i, 0), pipeline_mode=pl.Buffered(1)),
            pl.BlockSpec((1, N_ADA, d), lambda i, j: (i // per_batch, 0, 0)),
            pl.BlockSpec((1, d), lambda i, j: (0, 0)),
            pl.BlockSpec((d, tn), lambda i, j: (0, jnp.minimum(j, n_main - 1))),
            pl.BlockSpec((d, dtw), lambda i, j: (0, 0)),
            pl.BlockSpec((1, dtw), lambda i, j: (0, 0)),
        ],
        out_specs=[
            pl.BlockSpec((tm, tn), lambda i, j: (i, jnp.minimum(j, n_zx - 1))),
            pl.BlockSpec((tm, tn), lambda i, j: (i, jnp.clip(j - n_zx, 0, n_gate - 1))),
            pl.BlockSpec((tm, tn // 2), lambda i, j: (i, jnp.clip(j - n_zx - n_gate, 0, n_glu - 1))),
            pl.BlockSpec((tm, dtw), lambda i, j: (i, 0)),
        ],
        out_shape=[
            jax.ShapeDtypeStruct((m, zx_w), F32),
            jax.ShapeDtypeStruct((m, gate_w), F32),
            jax.ShapeDtypeStruct((m, glu_w), F32),
            jax.ShapeDtypeStruct((m, dtw), F32),
        ],
        scratch_shapes=[pltpu.VMEM((tm, d), BF16)],
        compiler_params=_params(("parallel", "arbitrary")),
        name="inproj",
    )(x2, ada, nw.reshape(1, d), w_main, w_dt, dt_b)


def _halo_rows(taps):
    return -(-(taps - 1) // SUBLANES) * SUBLANES


def _load_tile_with_halo(u_ref, ext_ref, first, halo, rows):
    @pl.when(first)
    def _():
        ext_ref[0:halo, :] = jnp.zeros((halo, ext_ref.shape[1]), F32)

    @pl.when(jnp.logical_not(first))
    def _():
        ext_ref[0:halo, :] = ext_ref[rows:rows + halo, :]

    ext_ref[halo:halo + rows, :] = u_ref[0]


def _conv_lane_block(ext_ref, sh_ref, w_ref, b_ref, cols, taps, halo, rows, rb, emit):
    base = halo - (taps - 1)
    span = halo + rows - SUBLANES
    for s in sorted({(base + k) % SUBLANES for k in range(taps)} - {0}):
        sh_ref[s, 0:span, :] = ext_ref[s:s + span, cols]
    for r0 in range(0, rows, rb):
        acc = None
        for k in range(taps):
            s = (base + k) % SUBLANES
            a = base + k - s + r0
            win = ext_ref[a:a + rb, cols] if s == 0 else sh_ref[s, a:a + rb, :]
            term = w_ref[k:k + 1, cols] * win
            acc = term if acc is None else acc + term
        emit(r0, acc + b_ref[:, cols])


def _ssm_conv_body(u_ref, w_ref, b_ref, o_ref, ext_ref, sh_ref, *, taps, rb, cb):
    rows, width = u_ref.shape[1], u_ref.shape[2]
    halo = _halo_rows(taps)
    _load_tile_with_halo(u_ref, ext_ref, pl.program_id(2) == 0, halo, rows)
    for c0 in range(0, width, cb):
        cols = slice(c0, c0 + cb)

        def emit(r0, acc, cols=cols):
            o_ref[0, r0:r0 + rb, cols] = _silu(acc)

        _conv_lane_block(ext_ref, sh_ref, w_ref, b_ref, cols, taps, halo, rows, rb, emit)


def _ssm_conv(zx3, col_off, conv_w, conv_b):
    bsz, seq, _ = zx3.shape
    taps, c = conv_w.shape
    tc = _pick(c, 512)
    tr = _pick(seq, 512)
    cb = _pick(tc, 256)
    assert col_off % tc == 0
    off = col_off // tc
    halo = _halo_rows(taps)
    return pl.pallas_call(
        functools.partial(_ssm_conv_body, taps=taps, rb=_pick(tr, 64), cb=cb),
        grid=(bsz, c // tc, seq // tr),
        in_specs=[
            pl.BlockSpec((1, tr, tc), lambda b, j, t: (b, t, off + j)),
            pl.BlockSpec((taps, tc), lambda b, j, t: (0, j)),
            pl.BlockSpec((1, tc), lambda b, j, t: (0, j)),
        ],
        out_specs=pl.BlockSpec((1, tr, tc), lambda b, j, t: (b, t, j)),
        out_shape=jax.ShapeDtypeStruct((bsz, seq, c), F32),
        scratch_shapes=[pltpu.VMEM((tr + halo, tc), F32),
                        pltpu.VMEM((SUBLANES, tr + halo, cb), F32)],
        compiler_params=_params(("parallel", "parallel", "arbitrary")),
        name="ssm_conv",
    )(zx3, conv_w, conv_b.reshape(1, c))


def _conf_conv_body(u_ref, w_ref, b_ref, g_ref, beta_ref, o_ref, ext_ref, sh_ref, y_ref,
                    *, taps, rb, cb):
    rows, width = u_ref.shape[1], u_ref.shape[2]
    halo = _halo_rows(taps)
    _load_tile_with_halo(u_ref, ext_ref, pl.program_id(1) == 0, halo, rows)

    def lane_block(ci, carry):
        cols = pl.ds(pl.multiple_of(ci * cb, cb), cb)

        def emit(r0, acc):
            y_ref[r0:r0 + rb, cols] = acc

        _conv_lane_block(ext_ref, sh_ref, w_ref, b_ref, cols, taps, halo, rows, rb, emit)
        return carry

    lax.fori_loop(0, width // cb, lane_block, 0)
    y = y_ref[...]
    mu = jnp.mean(y, axis=-1, keepdims=True)
    yc = y - mu
    var = jnp.mean(yc * yc, axis=-1, keepdims=True)
    yn = yc * lax.rsqrt(var + EPS) * g_ref[...] + beta_ref[...]
    o_ref[0] = _silu(yn).astype(BF16)


def _conf_conv(u3, dw_w, dw_b, ln_g, ln_b):
    bsz, seq, c = u3.shape
    taps = dw_w.shape[0]
    tr = _pick(seq, 256)
    cb = _pick(c, 256)
    halo = _halo_rows(taps)
    vec = lambda v: v.reshape(1, c)
    row_spec = pl.BlockSpec((1, c), lambda b, t: (0, 0))
    return pl.pallas_call(
        functools.partial(_conf_conv_body, taps=taps, rb=_pick(tr, 64), cb=cb),
        grid=(bsz, seq // tr),
        in_specs=[
            pl.BlockSpec((1, tr, c), lambda b, t: (b, t, 0)),
            pl.BlockSpec((taps, c), lambda b, t: (0, 0)),
            row_spec, row_spec, row_spec,
        ],
        out_specs=pl.BlockSpec((1, tr, c), lambda b, t: (b, t, 0)),
        out_shape=jax.ShapeDtypeStruct((bsz, seq, c), BF16),
        scratch_shapes=[pltpu.VMEM((tr + halo, c), F32),
                        pltpu.VMEM((SUBLANES, tr + halo, cb), F32),
                        pltpu.VMEM((tr, c), F32)],
        compiler_params=_params(("parallel", "arbitrary")),
        name="conf_conv",
    )(u3, dw_w, vec(dw_b), vec(ln_g), vec(ln_b))


def _split3(v):
    hi = v.astype(BF16)
    r1 = v - hi.astype(F32)
    mid = r1.astype(BF16)
    lo = (r1 - mid.astype(F32)).astype(BF16)
    return hi, mid, lo


def _ssd_body(xs_ref, b_ref, c_ref, z_ref, dt_ref, alog_ref, dsk_ref, nw_ref,
              yg_ref, ss_ref, s_ref, acs_ref, acst_ref, *, heads_per_group):
    t = pl.program_id(1)
    g = pl.program_id(2)
    q = SSD_CHUNK
    rows = xs_ref.shape[1]
    n_chunks = rows // q
    n_pairs = xs_ref.shape[2] // LANES
    causal = lax.broadcasted_iota(jnp.int32, (q, q), 0) >= lax.broadcasted_iota(jnp.int32, (q, q), 1)
    lo_half = lax.broadcasted_iota(jnp.int32, (q, LANES), 1) < HEAD_DIM

    @pl.when(t == 0)
    def _():
        s_ref[g] = jnp.zeros(s_ref.shape[1:], F32)

    @pl.when(g == 0)
    def _():
        ss_ref[...] = jnp.zeros(ss_ref.shape, F32)
        tri = causal.astype(BF16)
        a2 = -jnp.exp(alog_ref[...]) * LOG2E
        for ci in range(n_chunks):
            rs = slice(ci * q, (ci + 1) * q)
            hi, mid, lo = _split3(dt_ref[0, rs, :] * a2)
            cs = _dot(tri, jnp.concatenate([hi, mid, lo], axis=1))
            acs = (cs[:, :LANES] + cs[:, LANES:2 * LANES]) + cs[:, 2 * LANES:]
            acs_ref[rs, :] = acs
            acst_ref[ci] = acs.T

    to_lane0 = (LANES - g * heads_per_group) % LANES
    head_row0 = pl.multiple_of(g * heads_per_group, SUBLANES)

    pre = []
    for ci in range(n_chunks):
        rs = slice(ci * q, (ci + 1) * q)
        acs = pltpu.roll(acs_ref[rs, :], to_lane0, 1)
        dt = pltpu.roll(dt_ref[0, rs, :], to_lane0, 1)
        acs_t = acst_ref[ci, pl.ds(head_row0, SUBLANES), :]
        bg = b_ref[0, rs, :].astype(BF16)
        cg = c_ref[0, rs, :].astype(BF16)
        cb = lax.dot_general(cg, bg, (((1,), (1,)), ((), ())), preferred_element_type=F32)
        y_diag, off_scale, xw_parts, decay_parts = [], [], [], []
        for p in range(n_pairs):
            ls = slice(p * LANES, (p + 1) * LANES)
            h0, h1 = 2 * p, 2 * p + 1
            bc0 = jnp.broadcast_to(acs[:, h0:h0 + 1], (q, LANES))
            bc1 = jnp.broadcast_to(acs[:, h1:h1 + 1], (q, LANES))
            dt0 = jnp.broadcast_to(dt[:, h0:h0 + 1], (q, LANES))
            dt1 = jnp.broadcast_to(dt[:, h1:h1 + 1], (q, LANES))
            m0 = cb * jnp.exp2(jnp.where(causal, bc0 - acs_t[h0:h0 + 1, :], -jnp.inf))
            m1 = cb * jnp.exp2(jnp.where(causal, bc1 - acs_t[h1:h1 + 1, :], -jnp.inf))
            acs_pair = jnp.where(lo_half, bc0, bc1)
            xdt = xs_ref[0, rs, ls] * jnp.where(lo_half, dt0, dt1)
            lhs = jnp.concatenate([m0.astype(BF16), m1.astype(BF16)], axis=1)
            rhs = jnp.concatenate([jnp.where(lo_half, xdt, 0.0).astype(BF16),
                                   jnp.where(lo_half, 0.0, xdt).astype(BF16)], axis=0)
            y_diag.append(_dot(lhs, rhs))
            off_scale.append(jnp.exp2(acs_pair))
            last_pair = acs_pair[q - 1:q, :]
            xw_parts.append((xdt * jnp.exp2(last_pair - acs_pair)).astype(BF16))
            decay_parts.append(jnp.exp2(last_pair))
        xw = jnp.concatenate(xw_parts, axis=1)
        new = lax.dot_general(bg, xw, (((0,), (0,)), ((), ())), preferred_element_type=F32)
        pre.append((cg, y_diag, off_scale, jnp.concatenate(decay_parts, axis=1), new))

    state = s_ref[g]
    for ci in range(n_chunks):
        rs = slice(ci * q, (ci + 1) * q)
        cg, y_diag, off_scale, decay, new = pre[ci]
        y_off = _dot(cg, state.astype(BF16))
        state = state * decay + new
        ss_acc = jnp.zeros((q, 1), F32)
        for p in range(n_pairs):
            ls = slice(p * LANES, (p + 1) * LANES)
            y = y_diag[p] + y_off[:, ls] * off_scale[p] + dsk_ref[0, :, ls] * xs_ref[0, rs, ls]
            v = y * _silu(z_ref[0, rs, ls])
            ss_acc = ss_acc + jnp.sum(v * v, axis=-1, keepdims=True)
            yg_ref[0, rs, ls] = (v * nw_ref[:, ls]).astype(BF16)
        ss_ref[0, rs, :] += jnp.broadcast_to(ss_acc, (q, LANES))
    s_ref[g] = state


def _ssd(xbc3, zx3, dt3, a_pad, dsk, nw, d_inner):
    bsz, seq, _ = xbc3.shape
    groups = SSM_GROUPS
    gw = d_inner // groups
    n_state = (xbc3.shape[2] - d_inner) // (2 * groups)
    hpg = gw // HEAD_DIM
    assert gw % LANES == 0 and d_inner % n_state == 0 and hpg == SUBLANES
    assert dt3.shape[2] == LANES and groups * hpg <= LANES
    tr = _pick(seq, 512)
    assert tr % SSD_CHUNK == 0
    b_off = d_inner // n_state
    return pl.pallas_call(
        functools.partial(_ssd_body, heads_per_group=hpg),
        grid=(bsz, seq // tr, groups),
        in_specs=[
            pl.BlockSpec((1, tr, gw), lambda b, t, g: (b, t, g)),
            pl.BlockSpec((1, tr, n_state), lambda b, t, g: (b, t, b_off + g)),
            pl.BlockSpec((1, tr, n_state), lambda b, t, g: (b, t, b_off + groups + g)),
            pl.BlockSpec((1, tr, gw), lambda b, t, g: (b, t, g)),
            pl.BlockSpec((1, tr, LANES), lambda b, t, g: (b, t, 0)),
            pl.BlockSpec((1, LANES), lambda b, t, g: (0, 0)),
            pl.BlockSpec((1, 1, gw), lambda b, t, g: (g, 0, 0)),
            pl.BlockSpec((1, gw), lambda b, t, g: (0, g)),
        ],
        out_specs=[
            pl.BlockSpec((1, tr, gw), lambda b, t, g: (b, t, g)),
            pl.BlockSpec((1, tr, LANES), lambda b, t, g: (b, t, 0)),
        ],
        out_shape=[
            jax.ShapeDtypeStruct((bsz, seq, d_inner), BF16),
            jax.ShapeDtypeStruct((bsz, seq, LANES), F32),
        ],
        scratch_shapes=[pltpu.VMEM((groups, n_state, gw), F32),
                        pltpu.VMEM((tr, LANES), F32),
                        pltpu.VMEM((tr // SSD_CHUNK, LANES, SSD_CHUNK), F32)],
        compiler_params=_params(("parallel", "arbitrary", "arbitrary")),
        name="ssd",
    )(xbc3, xbc3, xbc3, zx3, dt3, a_pad, dsk, nw.reshape(1, d_inner))


def _merge_body(yg_ref, ss_ref, uc_ref, gs_ref, gc_ref, x_ref, ada_ref, wso_ref, wpw_ref,
                bpw_ref, wo_ref, o_ref, *, ada_base, d_inner):
    j = pl.program_id(1)

    @pl.when(j == 0)
    def _():
        o_ref[...] = jnp.zeros(o_ref.shape, F32)

    row_scale = lax.rsqrt(ss_ref[:, 0:1] * (1.0 / d_inner) + EPS)
    y_ssm = _dot(yg_ref[...], wso_ref[...]) * row_scale
    y_conv = _dot(uc_ref[...], wpw_ref[...]) + bpw_ref[...]
    merged = (gs_ref[...] * y_ssm + gc_ref[...] * y_conv).astype(BF16)
    o_ref[...] += _dot(merged, wo_ref[...])

    @pl.when(j == pl.num_programs(1) - 1)
    def _():
        o_ref[...] = x_ref[...] + ada_ref[0, ada_base + 2:ada_base + 3, :] * o_ref[...]


def _merge(yg2, ss2, uc2, gates2, x2, ada, ada_base, wso, wpw, bpw, wo, seq):
    m, d = x2.shape
    d_inner = yg2.shape[1]
    c = uc2.shape[1]
    tm = _pick(seq, 512)
    tn = _pick(d, 512)
    nn = d // tn
    per_batch = seq // tm
    return pl.pallas_call(
        functools.partial(_merge_body, ada_base=ada_base, d_inner=d_inner),
        grid=(m // tm, nn),
        in_specs=[
            pl.BlockSpec((tm, d_inner), lambda i, j: (i, 0)),
            pl.BlockSpec((tm, LANES), lambda i, j: (i, 0)),
            pl.BlockSpec((tm, c), lambda i, j: (i, 0)),
            pl.BlockSpec((tm, tn), lambda i, j: (i, j)),
            pl.BlockSpec((tm, tn), lambda i, j: (i, j + nn)),
            pl.BlockSpec((tm, d), lambda i, j: (i, 0)),
            pl.BlockSpec((1, N_ADA, d), lambda i, j: (i // per_batch, 0, 0)),
            pl.BlockSpec((d_inner, tn), lambda i, j: (0, j)),
            pl.BlockSpec((c, tn), lambda i, j: (0, j)),
            pl.BlockSpec((1, tn), lambda i, j: (0, j)),
            pl.BlockSpec((tn, d), lambda i, j: (j, 0)),
        ],
        out_specs=pl.BlockSpec((tm, d), lambda i, j: (i, 0)),
        out_shape=jax.ShapeDtypeStruct((m, d), F32),
        compiler_params=_params(("parallel", "arbitrary")),
        name="merge",
    )(yg2, ss2, uc2, gates2, gates2, x2, ada, wso, wpw, bpw.reshape(1, d), wo)


def _pad_lanes(v):
    return jnp.pad(v, [(0, 0)] * (v.ndim - 1) + [(0, LANES - v.shape[-1])])


def _interleave_tiles(a, b, half):
    d, n = a.shape
    return jnp.stack([a.reshape(d, n // half, half), b.reshape(d, n // half, half)],
                     axis=2).reshape(d, 2 * n)


def kernel(x, c, w_ada, b_ada, norm_ffn1, ffn1_w13, ffn1_w2, norm_mix, w_in, ssm_conv_w, ssm_conv_b, dt_bias, a_log, d_skip, ssm_norm_w, w_ssm_out, dw_w, dw_b, conv_ln_g, conv_ln_b, w_pw2, b_pw2, w_o, norm_ffn2, ffn2_w13, ffn2_w2, final_norm):
    bsz, seq, d = x.shape
    depth = w_ada.shape[0]
    groups = SSM_GROUPS
    d_inner = w_ssm_out.shape[1]
    conv_dim = ssm_conv_w.shape[2]
    heads = dt_bias.shape[1]
    conf_c = dw_w.shape[2]
    assert d_inner // heads == HEAD_DIM
    p_dt = d_inner + conv_dim
    p_glu = p_dt + heads
    p_gate = p_glu + 2 * conf_c
    gate_w = w_in.shape[2] - p_gate
    tn = _pick(math.gcd(p_dt, gate_w, 2 * conf_c), 1024)

    ada_all = _ada(c, w_ada, b_ada)
    x2 = x.reshape(bsz * seq, d)
    for l in range(depth):
        ada = ada_all[l]
        wl = w_in[l]
        w_glu = _interleave_tiles(wl[:, p_glu:p_glu + conf_c], wl[:, p_glu + conf_c:p_gate], tn // 2)
        w_main = jnp.concatenate([wl[:, :p_dt], wl[:, p_gate:], w_glu], axis=1).astype(BF16)
        w_dt = _pad_lanes(wl[:, p_dt:p_glu]).astype(BF16)
        dt_b = _pad_lanes(dt_bias[l]).reshape(1, LANES)
        a_pad = _pad_lanes(a_log[l]).reshape(1, LANES)
        dsk = jnp.repeat(d_skip[l], HEAD_DIM).reshape(groups, 1, d_inner // groups)

        x2 = _ffn(x2, ada, 0, norm_ffn1[l], ffn1_w13[l].astype(BF16), ffn1_w2[l].astype(BF16),
                  final_norm, seq, False)

        zx, gates, u, dt = _inproj(x2, ada, 3, norm_mix[l], w_main, w_dt, dt_b,
                                   (p_dt, gate_w, conf_c), seq, tn)

        zx3 = zx.reshape(bsz, seq, p_dt)
        xbc = _ssm_conv(zx3, d_inner, ssm_conv_w[l], ssm_conv_b[l])
        yg, ss = _ssd(xbc, zx3, dt.reshape(bsz, seq, LANES), a_pad, dsk, ssm_norm_w[l], d_inner)
        uc = _conf_conv(u.reshape(bsz, seq, conf_c), dw_w[l], dw_b[l], conv_ln_g[l], conv_ln_b[l])

        x2 = _merge(yg.reshape(bsz * seq, d_inner), ss.reshape(bsz * seq, LANES),
                    uc.reshape(bsz * seq, conf_c), gates, x2, ada, 3,
                    w_ssm_out[l].astype(BF16), w_pw2[l].astype(BF16), b_pw2[l],
                    w_o[l].astype(BF16), seq)

        x2 = _ffn(x2, ada, 6, norm_ffn2[l], ffn2_w13[l].astype(BF16), ffn2_w2[l].astype(BF16),
                  final_norm, seq, l == depth - 1)
    return x2.reshape(bsz, seq, d)
```

```python
import functools
import math

import jax
import jax.numpy as jnp
from jax import lax
from jax.experimental import pallas as pl
from jax.experimental.pallas import tpu as pltpu

F32 = jnp.float32
BF16 = jnp.bfloat16
EPS = 1e-6
LOG2E = 1.4426950408889634

SSM_GROUPS = 8
N_ADA = 9

LANES = 128
SUBLANES = 8
VMEM_LIMIT_BYTES = 60 * 1024 * 1024

SSD_CHUNK = 128
HEAD_DIM = 64
CAST_BLOCK_BYTES = 8 * 1024 * 1024


def _params(semantics):
    return pltpu.CompilerParams(dimension_semantics=semantics,
                                vmem_limit_bytes=VMEM_LIMIT_BYTES)


def _pick(total, want):
    t = min(total, want)
    while total % t:
        t //= 2
    return t


def _silu(v):
    return v * jax.nn.sigmoid(v)


def _softplus(v):
    return jnp.maximum(v, 0.0) + jnp.log1p(jnp.exp(-jnp.abs(v)))


def _rms_mod(x, nw, shift, scale):
    y = x * lax.rsqrt(jnp.mean(x * x, axis=-1, keepdims=True) + EPS) * nw
    return y * (1.0 + scale) + shift


def _dot(a, b):
    return jnp.dot(a, b, preferred_element_type=F32)


def _cast_body(w_ref, o_ref):
    o_ref[...] = w_ref[...].astype(BF16)


def _to_bf16(w):
    depth, rows, cols = w.shape
    want = 1 << int(math.log2(max(SUBLANES, CAST_BLOCK_BYTES // (4 * cols))))
    tr = _pick(rows, want)
    out = pl.pallas_call(
        _cast_body,
        grid=(depth * rows // tr,),
        in_specs=[pl.BlockSpec((tr, cols), lambda i: (i, 0))],
        out_specs=pl.BlockSpec((tr, cols), lambda i: (i, 0)),
        out_shape=jax.ShapeDtypeStruct((depth * rows, cols), BF16),
        compiler_params=_params(("parallel",)),
        name="cast",
    )(w.reshape(depth * rows, cols))
    return out.reshape(depth, rows, cols)


def _ada_body(c_ref, w_ref, b_ref, o_ref):
    c = c_ref[...]
    o_ref[0] = _dot(_silu(c).astype(BF16), w_ref[0].astype(BF16)) + b_ref[0]


def _ada(c, w_ada, b_ada):
    depth, d, n = w_ada.shape
    bsz = c.shape[0]
    rows = -(-bsz // SUBLANES) * SUBLANES
    c_pad = jnp.pad(c, ((0, rows - bsz), (0, 0)))
    tn = _pick(n, 1024)
    out = pl.pallas_call(
        _ada_body,
        grid=(depth, n // tn),
        in_specs=[
            pl.BlockSpec((rows, d), lambda l, j: (0, 0)),
            pl.BlockSpec((1, d, tn), lambda l, j: (l, 0, j)),
            pl.BlockSpec((1, 1, tn), lambda l, j: (l, 0, j)),
        ],
        out_specs=pl.BlockSpec((1, rows, tn), lambda l, j: (l, 0, j)),
        out_shape=jax.ShapeDtypeStruct((depth, rows, n), F32),
        compiler_params=_params(("parallel", "parallel")),
        name="ada",
    )(c_pad, w_ada, b_ada.reshape(depth, 1, n))
    return out[:, :bsz].reshape(depth, bsz, N_ADA, d)


def _ffn_body(x_ref, ada_ref, nw_ref, w1_ref, w3_ref, w2_ref, fw_ref, o_ref, h_ref,
              *, ada_base, final):
    j = pl.program_id(1)

    @pl.when(j == 0)
    def _():
        shift = ada_ref[0, ada_base:ada_base + 1, :]
        scale = ada_ref[0, ada_base + 1:ada_base + 2, :]
        h_ref[...] = _rms_mod(x_ref[...], nw_ref[...], shift, scale).astype(BF16)
        o_ref[...] = jnp.zeros(o_ref.shape, F32)

    h = h_ref[...]
    a = _dot(h, w1_ref[...])
    g = _dot(h, w3_ref[...])
    u = (_silu(g) * a).astype(BF16)
    o_ref[...] += _dot(u, w2_ref[...])

    @pl.when(j == pl.num_programs(1) - 1)
    def _():
        gate = ada_ref[0, ada_base + 2:ada_base + 3, :]
        y = x_ref[...] + 0.5 * gate * o_ref[...]
        if final:
            y = y * lax.rsqrt(jnp.mean(y * y, axis=-1, keepdims=True) + EPS) * fw_ref[...]
        o_ref[...] = y


def _ffn(x2, ada, ada_base, nw, w13, w2, layer, fw, seq, final):
    m, d = x2.shape
    f = w2.shape[1]
    tm = _pick(seq, 1024)
    tf = _pick(f, 512)
    nf = f // tf
    per_batch = seq // tm
    return pl.pallas_call(
        functools.partial(_ffn_body, ada_base=ada_base, final=final),
        grid=(m // tm, nf),
        in_specs=[
            pl.BlockSpec((tm, d), lambda i, j: (i, 0), pipeline_mode=pl.Buffered(1)),
            pl.BlockSpec((1, N_ADA, d), lambda i, j: (i // per_batch, 0, 0)),
            pl.BlockSpec((1, d), lambda i, j: (0, 0)),
            pl.BlockSpec((None, d, tf), lambda i, j: (layer, 0, j)),
            pl.BlockSpec((None, d, tf), lambda i, j: (layer, 0, j + nf)),
            pl.BlockSpec((None, tf, d), lambda i, j: (layer, j, 0)),
            pl.BlockSpec((1, d), lambda i, j: (0, 0)),
        ],
        out_specs=pl.BlockSpec((tm, d), lambda i, j: (i, 0)),
        out_shape=jax.ShapeDtypeStruct((m, d), F32),
        scratch_shapes=[pltpu.VMEM((tm, d), BF16)],
        compiler_params=_params(("parallel", "arbitrary")),
        name="ffn",
    )(x2, ada, nw.reshape(1, d), w13, w13, w2, fw.reshape(1, d))


def _halo_rows(taps):
    return -(-(taps - 1) // SUBLANES) * SUBLANES


def _shift_residues(taps):
    base = _halo_rows(taps) - (taps - 1)
    return sorted({(base + k) % SUBLANES for k in range(taps)} - {0})


def _conv_lane_block(ext_ref, sh_ref, w_ref, b_ref, cols, taps, rows, rb, emit):
    halo = _halo_rows(taps)
    base = halo - (taps - 1)
    span = halo + rows - SUBLANES
    slot = {s: i for i, s in enumerate(_shift_residues(taps))}
    for s, i in slot.items():
        sh_ref[i, 0:span, :] = ext_ref[s:s + span, cols]
    for r0 in range(0, rows, rb):
        acc = None
        for k in range(taps):
            s = (base + k) % SUBLANES
            a = base + k - s + r0
            win = ext_ref[a:a + rb, cols] if s == 0 else sh_ref[slot[s], a:a + rb, :]
            term = w_ref[k:k + 1, cols] * win
            acc = term if acc is None else acc + term
        emit(r0, acc + b_ref[:, cols])


def _inproj_body(x_ref, ada_ref, nw_ref, w_ref, wdt_ref, dtb_ref, cw_ref, cb_ref,
                 zx_ref, gate_ref, u_ref, dt_ref, h_ref, ext_ref, sh_ref, carry_ref,
                 *, ada_base, n_z, n_x, n_gate, n_glu, per_batch, taps, rb, lb):
    i = pl.program_id(0)
    j = pl.program_id(1)
    n_zx = n_z + n_x
    n_main = n_zx + n_gate + n_glu
    tm, tn = zx_ref.shape
    halo = _halo_rows(taps)

    @pl.when(j == 0)
    def _():
        shift = ada_ref[0, ada_base:ada_base + 1, :]
        scale = ada_ref[0, ada_base + 1:ada_base + 2, :]
        h_ref[...] = _rms_mod(x_ref[...], nw_ref[...], shift, scale).astype(BF16)

    @pl.when(j < n_z)
    def _():
        zx_ref[...] = _dot(h_ref[...], w_ref[...])

    @pl.when(jnp.logical_and(j >= n_z, j < n_zx))
    def _():
        xt = j - n_z

        @pl.when(i % per_batch == 0)
        def _():
            ext_ref[0:halo, :] = jnp.zeros((halo, tn), F32)

        @pl.when(i % per_batch != 0)
        def _():
            ext_ref[0:halo, :] = carry_ref[xt]

        ext_ref[halo:halo + tm, :] = _dot(h_ref[...], w_ref[...])
        carry_ref[xt] = ext_ref[tm:tm + halo, :]
        for c0 in range(0, tn, lb):
            cols = slice(c0, c0 + lb)

            def emit(r0, acc, cols=cols):
                zx_ref[r0:r0 + rb, cols] = _silu(acc)

            _conv_lane_block(ext_ref, sh_ref, cw_ref, cb_ref, cols, taps, tm, rb, emit)

    @pl.when(jnp.logical_and(j >= n_zx, j < n_zx + n_gate))
    def _():
        gate_ref[...] = jax.nn.sigmoid(_dot(h_ref[...], w_ref[...]))

    @pl.when(jnp.logical_and(j >= n_zx + n_gate, j < n_main))
    def _():
        acc = _dot(h_ref[...], w_ref[...])
        half = acc.shape[1] // 2
        u_ref[...] = acc[:, :half] * jax.nn.sigmoid(acc[:, half:])

    @pl.when(j == n_main)
    def _():
        dt_ref[...] = _softplus(_dot(h_ref[...], wdt_ref[...]) + dtb_ref[...])


def _inproj(x2, ada, ada_base, nw, w_main, w_dt, dt_b, conv_w, conv_b, widths, seq, tn):
    m, d = x2.shape
    z_w, gate_w, glu_w = widths
    taps, x_w = conv_w.shape
    n_z, n_x, n_gate, n_glu = z_w // tn, x_w // tn, gate_w // tn, 2 * glu_w // tn
    n_zx = n_z + n_x
    n_main = n_zx + n_gate + n_glu
    tm = _pick(seq, 1024)
    per_batch = seq // tm
    dtw = w_dt.shape[1]
    halo = _halo_rows(taps)
    lb = _pick(tn, 256)
    body = functools.partial(_inproj_body, ada_base=ada_base, n_z=n_z, n_x=n_x, n_gate=n_gate,
                             n_glu=n_glu, per_batch=per_batch, taps=taps, rb=_pick(tm, 64), lb=lb)
    conv_tile = lambda i, j: (0, jnp.clip(j - n_z, 0, n_x - 1))
    return pl.pallas_call(
        body,
        grid=(m // tm, n_main + 1),
        in_specs=[
            pl.BlockSpec((tm, d), lambda i, j: (i, 0), pipeline_mode=pl.Buffered(1)),
            pl.BlockSpec((1, N_ADA, d), lambda i, j: (i // per_batch, 0, 0)),
            pl.BlockSpec((1, d), lambda i, j: (0, 0)),
            pl.BlockSpec((d, tn), lambda i, j: (0, jnp.minimum(j, n_main - 1))),
            pl.BlockSpec((d, dtw), lambda i, j: (0, 0)),
            pl.BlockSpec((1, dtw), lambda i, j: (0, 0)),
            pl.BlockSpec((taps, tn), conv_tile),
            pl.BlockSpec((1, tn), conv_tile),
        ],
        out_specs=[
            pl.BlockSpec((tm, tn), lambda i, j: (i, jnp.minimum(j, n_zx - 1))),
            pl.BlockSpec((tm, tn), lambda i, j: (i, jnp.clip(j - n_zx, 0, n_gate - 1))),
            pl.BlockSpec((tm, tn // 2), lambda i, j: (i, jnp.clip(j - n_zx - n_gate, 0, n_glu - 1))),
            pl.BlockSpec((tm, dtw), lambda i, j: (i, 0)),
        ],
        out_shape=[
            jax.ShapeDtypeStruct((m, z_w + x_w), F32),
            jax.ShapeDtypeStruct((m, gate_w), F32),
            jax.ShapeDtypeStruct((m, glu_w), F32),
            jax.ShapeDtypeStruct((m, dtw), F32),
        ],
        scratch_shapes=[pltpu.VMEM((tm, d), BF16),
                        pltpu.VMEM((halo + tm, tn), F32),
                        pltpu.VMEM((len(_shift_residues(taps)), halo + tm, lb), F32),
                        pltpu.VMEM((n_x, halo, tn), F32)],
        compiler_params=_params(("arbitrary", "arbitrary")),
        name="inproj",
    )(x2, ada, nw.reshape(1, d), w_main, w_dt, dt_b, conv_w, conv_b.reshape(1, x_w))


def _conf_conv_body(u_ref, w_ref, b_ref, g_ref, beta_ref, o_ref, ext_ref, sh_ref, y_ref,
                    *, taps, rb, cb):
    rows, width = u_ref.shape[1], u_ref.shape[2]
    halo = _halo_rows(taps)
    first = pl.program_id(1) == 0

    @pl.when(first)
    def _():
        ext_ref[0:halo, :] = jnp.zeros((halo, width), F32)

    @pl.when(jnp.logical_not(first))
    def _():
        ext_ref[0:halo, :] = ext_ref[rows:rows + halo, :]

    ext_ref[halo:halo + rows, :] = u_ref[0]

    def lane_block(ci, carry):
        cols = pl.ds(pl.multiple_of(ci * cb, cb), cb)

        def emit(r0, acc):
            y_ref[r0:r0 + rb, cols] = acc

        _conv_lane_block(ext_ref, sh_ref, w_ref, b_ref, cols, taps, rows, rb, emit)
        return carry

    lax.fori_loop(0, width // cb, lane_block, 0)
    y = y_ref[...]
    mu = jnp.mean(y, axis=-1, keepdims=True)
    yc = y - mu
    var = jnp.mean(yc * yc, axis=-1, keepdims=True)
    yn = yc * lax.rsqrt(var + EPS) * g_ref[...] + beta_ref[...]
    o_ref[0] = _silu(yn).astype(BF16)


def _conf_conv(u3, dw_w, dw_b, ln_g, ln_b):
    bsz, seq, c = u3.shape
    taps = dw_w.shape[0]
    tr = _pick(seq, 256)
    cb = _pick(c, 256)
    halo = _halo_rows(taps)
    vec = lambda v: v.reshape(1, c)
    row_spec = pl.BlockSpec((1, c), lambda b, t: (0, 0))
    return pl.pallas_call(
        functools.partial(_conf_conv_body, taps=taps, rb=_pick(tr, 64), cb=cb),
        grid=(bsz, seq // tr),
        in_specs=[
            pl.BlockSpec((1, tr, c), lambda b, t: (b, t, 0)),
            pl.BlockSpec((taps, c), lambda b, t: (0, 0)),
            row_spec, row_spec, row_spec,
        ],
        out_specs=pl.BlockSpec((1, tr, c), lambda b, t: (b, t, 0)),
        out_shape=jax.ShapeDtypeStruct((bsz, seq, c), BF16),
        scratch_shapes=[pltpu.VMEM((tr + halo, c), F32),
                        pltpu.VMEM((len(_shift_residues(taps)), tr + halo, cb), F32),
                        pltpu.VMEM((tr, c), F32)],
        compiler_params=_params(("parallel", "arbitrary")),
        name="conf_conv",
    )(u3, dw_w, vec(dw_b), vec(ln_g), vec(ln_b))


def _split3(v):
    hi = v.astype(BF16)
    r1 = v - hi.astype(F32)
    mid = r1.astype(BF16)
    lo = (r1 - mid.astype(F32)).astype(BF16)
    return hi, mid, lo


def _ssd_body(xs_ref, b_ref, c_ref, z_ref, dt_ref, alog_ref, dsk_ref, nw_ref,
              yg_ref, ss_ref, s_ref, acs_ref, acst_ref, *, heads_per_group):
    t = pl.program_id(1)
    g = pl.program_id(2)
    q = SSD_CHUNK
    rows = xs_ref.shape[1]
    n_chunks = rows // q
    n_pairs = xs_ref.shape[2] // LANES
    causal = lax.broadcasted_iota(jnp.int32, (q, q), 0) >= lax.broadcasted_iota(jnp.int32, (q, q), 1)
    lane = lax.broadcasted_iota(jnp.int32, (q, LANES), 1)
    lo_half = lane < HEAD_DIM

    @pl.when(t == 0)
    def _():
        s_ref[g] = jnp.zeros(s_ref.shape[1:], F32)

    @pl.when(g == 0)
    def _():
        ss_ref[...] = jnp.zeros(ss_ref.shape, F32)
        tri = causal.astype(BF16)
        a2 = -jnp.exp(alog_ref[...]) * LOG2E
        for ci in range(n_chunks):
            rs = slice(ci * q, (ci + 1) * q)
            hi, mid, lo = _split3(dt_ref[0, rs, :] * a2)
            cs = _dot(tri, jnp.concatenate([hi, mid, lo], axis=1))
            acs = (cs[:, :LANES] + cs[:, LANES:2 * LANES]) + cs[:, 2 * LANES:]
            acs_ref[rs, :] = acs
            acst_ref[ci] = acs.T

    to_lane0 = (LANES - g * heads_per_group) % LANES
    head_row0 = pl.multiple_of(g * heads_per_group, SUBLANES)

    pre = []
    for ci in range(n_chunks):
        rs = slice(ci * q, (ci + 1) * q)
        acs = pltpu.roll(acs_ref[rs, :], to_lane0, 1)
        dt = pltpu.roll(dt_ref[0, rs, :], to_lane0, 1)
        acs_t = acst_ref[ci, pl.ds(head_row0, SUBLANES), :]
        bg = b_ref[0, rs, :].astype(BF16)
        cg = c_ref[0, rs, :].astype(BF16)
        cb = lax.dot_general(cg, bg, (((1,), (1,)), ((), ())), preferred_element_type=F32)
        y_diag, off_scale, xw_parts, decay_parts = [], [], [], []
        for p in range(n_pairs):
            ls = slice(p * LANES, (p + 1) * LANES)
            h0, h1 = 2 * p, 2 * p + 1
            bc0 = jnp.broadcast_to(acs[:, h0:h0 + 1], (q, LANES))
            bc1 = jnp.broadcast_to(acs[:, h1:h1 + 1], (q, LANES))
            m0 = cb * jnp.exp2(jnp.where(causal, bc0 - acs_t[h0:h0 + 1, :], -jnp.inf))
            m1 = cb * jnp.exp2(jnp.where(causal, bc1 - acs_t[h1:h1 + 1, :], -jnp.inf))
            acs_pair = jnp.where(lo_half, bc0, bc1)
            dt_pair = jnp.take_along_axis(dt, jnp.where(lo_half, h0, h1), axis=1)
            xdt = xs_ref[0, rs, ls] * dt_pair
            lhs = jnp.concatenate([m0.astype(BF16), m1.astype(BF16)], axis=1)
            rhs = jnp.concatenate([jnp.where(lo_half, xdt, 0.0).astype(BF16),
                                   jnp.where(lo_half, 0.0, xdt).astype(BF16)], axis=0)
            y_diag.append(_dot(lhs, rhs))
            off_scale.append(jnp.exp2(acs_pair))
            last_pair = acs_pair[q - 1:q, :]
            xw_parts.append((xdt * jnp.exp2(last_pair - acs_pair)).astype(BF16))
            decay_parts.append(jnp.exp2(last_pair))
        xw = jnp.concatenate(xw_parts, axis=1)
        new = lax.dot_general(bg, xw, (((0,), (0,)), ((), ())), preferred_element_type=F32)
        pre.append((cg, y_diag, off_scale, jnp.concatenate(decay_parts, axis=1), new))

    state = s_ref[g]
    for ci in range(n_chunks):
        rs = slice(ci * q, (ci + 1) * q)
        cg, y_diag, off_scale, decay, new = pre[ci]
        y_off = _dot(cg, state.astype(BF16))
        state = state * decay + new
        sq = jnp.zeros((q, LANES), F32)
        for p in range(n_pairs):
            ls = slice(p * LANES, (p + 1) * LANES)
            y = y_diag[p] + y_off[:, ls] * off_scale[p] + dsk_ref[0, :, ls] * xs_ref[0, rs, ls]
            v = y * _silu(z_ref[0, rs, ls])
            sq = sq + v * v
            yg_ref[0, rs, ls] = (v * nw_ref[:, ls]).astype(BF16)
        ss_ref[0, rs, :] += sq
    s_ref[g] = state


def _ssd(zx3, dt3, a_pad, dsk, nw, d_inner):
    bsz, seq, width = zx3.shape
    groups = SSM_GROUPS
    gw = d_inner // groups
    n_state = (width - 2 * d_inner) // (2 * groups)
    hpg = gw // HEAD_DIM
    assert gw % LANES == 0 and d_inner % n_state == 0 and hpg == SUBLANES
    assert dt3.shape[2] == LANES and groups * hpg <= LANES
    tr = _pick(seq, 512)
    assert tr % SSD_CHUNK == 0
    x_off = d_inner // gw
    b_off = 2 * d_inner // n_state
    return pl.pallas_call(
        functools.partial(_ssd_body, heads_per_group=hpg),
        grid=(bsz, seq // tr, groups),
        in_specs=[
            pl.BlockSpec((1, tr, gw), lambda b, t, g: (b, t, x_off + g)),
            pl.BlockSpec((1, tr, n_state), lambda b, t, g: (b, t, b_off + g)),
            pl.BlockSpec((1, tr, n_state), lambda b, t, g: (b, t, b_off + groups + g)),
            pl.BlockSpec((1, tr, gw), lambda b, t, g: (b, t, g)),
            pl.BlockSpec((1, tr, LANES), lambda b, t, g: (b, t, 0)),
            pl.BlockSpec((1, LANES), lambda b, t, g: (0, 0)),
            pl.BlockSpec((1, 1, gw), lambda b, t, g: (g, 0, 0)),
            pl.BlockSpec((1, gw), lambda b, t, g: (0, g)),
        ],
        out_specs=[
            pl.BlockSpec((1, tr, gw), lambda b, t, g: (b, t, g)),
            pl.BlockSpec((1, tr, LANES), lambda b, t, g: (b, t, 0)),
        ],
        out_shape=[
            jax.ShapeDtypeStruct((bsz, seq, d_inner), BF16),
            jax.ShapeDtypeStruct((bsz, seq, LANES), F32),
        ],
        scratch_shapes=[pltpu.VMEM((groups, n_state, gw), F32),
                        pltpu.VMEM((tr, LANES), F32),
                        pltpu.VMEM((tr // SSD_CHUNK, LANES, SSD_CHUNK), F32)],
        compiler_params=_params(("parallel", "arbitrary", "arbitrary")),
        name="ssd",
    )(zx3, zx3, zx3, zx3, dt3, a_pad, dsk, nw.reshape(1, d_inner))


def _merge_body(yg_ref, ss_ref, uc_ref, gs_ref, gc_ref, x_ref, ada_ref, wso_ref, wpw_ref,
                bpw_ref, wo_ref, o_ref, *, ada_base, d_inner):
    j = pl.program_id(1)

    @pl.when(j == 0)
    def _():
        o_ref[...] = jnp.zeros(o_ref.shape, F32)

    mean_sq = jnp.sum(ss_ref[...], axis=-1, keepdims=True) * (1.0 / d_inner)
    y_ssm = _dot(yg_ref[...], wso_ref[...]) * lax.rsqrt(mean_sq + EPS)
    y_conv = _dot(uc_ref[...], wpw_ref[...]) + bpw_ref[...]
    merged = (gs_ref[...] * y_ssm + gc_ref[...] * y_conv).astype(BF16)
    o_ref[...] += _dot(merged, wo_ref[...])

    @pl.when(j == pl.num_programs(1) - 1)
    def _():
        o_ref[...] = x_ref[...] + ada_ref[0, ada_base + 2:ada_base + 3, :] * o_ref[...]


def _merge(yg2, ss2, uc2, gates2, x2, ada, ada_base, wso, wpw, bpw, wo, layer, seq):
    m, d = x2.shape
    d_inner = yg2.shape[1]
    c = uc2.shape[1]
    tm = _pick(seq, 512)
    tn = _pick(d, 512)
    nn = d // tn
    per_batch = seq // tm
    return pl.pallas_call(
        functools.partial(_merge_body, ada_base=ada_base, d_inner=d_inner),
        grid=(m // tm, nn),
        in_specs=[
            pl.BlockSpec((tm, d_inner), lambda i, j: (i, 0)),
            pl.BlockSpec((tm, LANES), lambda i, j: (i, 0)),
            pl.BlockSpec((tm, c), lambda i, j: (i, 0)),
            pl.BlockSpec((tm, tn), lambda i, j: (i, j)),
            pl.BlockSpec((tm, tn), lambda i, j: (i, j + nn)),
            pl.BlockSpec((tm, d), lambda i, j: (i, 0)),
            pl.BlockSpec((1, N_ADA, d), lambda i, j: (i // per_batch, 0, 0)),
            pl.BlockSpec((None, d_inner, tn), lambda i, j: (layer, 0, j)),
            pl.BlockSpec((None, c, tn), lambda i, j: (layer, 0, j)),
            pl.BlockSpec((1, tn), lambda i, j: (0, j)),
            pl.BlockSpec((None, tn, d), lambda i, j: (layer, j, 0)),
        ],
        out_specs=pl.BlockSpec((tm, d), lambda i, j: (i, 0)),
        out_shape=jax.ShapeDtypeStruct((m, d), F32),
        compiler_params=_params(("parallel", "arbitrary")),
        name="merge",
    )(yg2, ss2, uc2, gates2, gates2, x2, ada, wso, wpw, bpw.reshape(1, d), wo)


def _pad_lanes(v):
    return jnp.pad(v, [(0, 0)] * (v.ndim - 1) + [(0, LANES - v.shape[-1])])


def _interleave_tiles(a, b, half):
    d, n = a.shape
    return jnp.stack([a.reshape(d, n // half, half), b.reshape(d, n // half, half)],
                     axis=2).reshape(d, 2 * n)


def kernel(x, c, w_ada, b_ada, norm_ffn1, ffn1_w13, ffn1_w2, norm_mix, w_in, ssm_conv_w, ssm_conv_b, dt_bias, a_log, d_skip, ssm_norm_w, w_ssm_out, dw_w, dw_b, conv_ln_g, conv_ln_b, w_pw2, b_pw2, w_o, norm_ffn2, ffn2_w13, ffn2_w2, final_norm):
    bsz, seq, d = x.shape
    depth = w_ada.shape[0]
    groups = SSM_GROUPS
    d_inner = w_ssm_out.shape[1]
    conv_dim = ssm_conv_w.shape[2]
    heads = dt_bias.shape[1]
    conf_c = dw_w.shape[2]
    assert d_inner // heads == HEAD_DIM
    p_dt = d_inner + conv_dim
    p_glu = p_dt + heads
    p_gate = p_glu + 2 * conf_c
    gate_w = w_in.shape[2] - p_gate
    tn = _pick(math.gcd(d_inner, conv_dim, gate_w, 2 * conf_c), 1024)

    ada_all = _ada(c, w_ada, b_ada)
    w13_1, w2_1 = _to_bf16(ffn1_w13), _to_bf16(ffn1_w2)
    w13_2, w2_2 = _to_bf16(ffn2_w13), _to_bf16(ffn2_w2)
    wso, wpw, wo = _to_bf16(w_ssm_out), _to_bf16(w_pw2), _to_bf16(w_o)
    x2 = x.reshape(bsz * seq, d)
    for l in range(depth):
        ada = ada_all[l]
        wl = w_in[l]
        w_glu = _interleave_tiles(wl[:, p_glu:p_glu + conf_c], wl[:, p_glu + conf_c:p_gate], tn // 2)
        w_main = jnp.concatenate([wl[:, :p_dt], wl[:, p_gate:], w_glu], axis=1).astype(BF16)
        w_dt = _pad_lanes(wl[:, p_dt:p_glu]).astype(BF16)
        dt_b = _pad_lanes(dt_bias[l]).reshape(1, LANES)
        a_pad = _pad_lanes(a_log[l]).reshape(1, LANES)
        dsk = jnp.repeat(d_skip[l], HEAD_DIM).reshape(groups, 1, d_inner // groups)

        x2 = _ffn(x2, ada, 0, norm_ffn1[l], w13_1, w2_1, l, final_norm, seq, False)

        zx, gates, u, dt = _inproj(x2, ada, 3, norm_mix[l], w_main, w_dt, dt_b,
                                   ssm_conv_w[l], ssm_conv_b[l], (d_inner, gate_w, conf_c), seq, tn)

        yg, ss = _ssd(zx.reshape(bsz, seq, p_dt), dt.reshape(bsz, seq, LANES), a_pad, dsk,
                      ssm_norm_w[l], d_inner)
        uc = _conf_conv(u.reshape(bsz, seq, conf_c), dw_w[l], dw_b[l], conv_ln_g[l], conv_ln_b[l])

        x2 = _merge(yg.reshape(bsz * seq, d_inner), ss.reshape(bsz * seq, LANES),
                    uc.reshape(bsz * seq, conf_c), gates, x2, ada, 3,
                    wso, wpw, b_pw2[l], wo, l, seq)

        x2 = _ffn(x2, ada, 6, norm_ffn2[l], w13_2, w2_2, l, final_norm, seq, l == depth - 1)
    return x2.reshape(bsz, seq, d)
```

```python
import functools
import math

import jax
import jax.numpy as jnp
from jax import lax
from jax.experimental import pallas as pl
from jax.experimental.pallas import tpu as pltpu

F32 = jnp.float32
BF16 = jnp.bfloat16
EPS = 1e-6
LOG2E = 1.4426950408889634

SSM_GROUPS = 8
N_ADA = 9

LANES = 128
SUBLANES = 8
VMEM_LIMIT_BYTES = 60 * 1024 * 1024

SSD_CHUNK = 128
HEAD_DIM = 64
CAST_BLOCK_BYTES = 8 * 1024 * 1024


def _params(semantics):
    return pltpu.CompilerParams(dimension_semantics=semantics,
                                vmem_limit_bytes=VMEM_LIMIT_BYTES)


def _pick(total, want):
    t = min(total, want)
    while total % t:
        t //= 2
    return t


def _silu(v):
    return v * jax.nn.sigmoid(v)


def _softplus(v):
    return jnp.maximum(v, 0.0) + jnp.log1p(jnp.exp(-jnp.abs(v)))


def _rms_mod(x, nw, shift, scale):
    y = x * lax.rsqrt(jnp.mean(x * x, axis=-1, keepdims=True) + EPS) * nw
    return y * (1.0 + scale) + shift


def _dot(a, b):
    return jnp.dot(a, b, preferred_element_type=F32)


def _cast_body(w_ref, o_ref):
    tc = o_ref.shape[-1]
    for t in range(o_ref.shape[0]):
        o_ref[t] = w_ref[:, t * tc:(t + 1) * tc].astype(BF16)


def _to_bf16(w, tc):
    depth, rows, cols = w.shape
    want = 1 << int(math.log2(max(SUBLANES, CAST_BLOCK_BYTES // (4 * cols))))
    tr = _pick(rows, want)
    nt = cols // tc
    per_layer = rows // tr
    return pl.pallas_call(
        _cast_body,
        grid=(depth * per_layer,),
        in_specs=[pl.BlockSpec((None, tr, cols), lambda i: (i // per_layer, i % per_layer, 0))],
        out_specs=pl.BlockSpec((None, nt, tr, tc), lambda i: (i // per_layer, 0, i % per_layer, 0)),
        out_shape=jax.ShapeDtypeStruct((depth, nt, rows, tc), BF16),
        compiler_params=_params(("parallel",)),
        name="cast",
    )(w)


def _ada_body(c_ref, w_ref, b_ref, o_ref):
    c = c_ref[...]
    o_ref[0] = _dot(_silu(c).astype(BF16), w_ref[0].astype(BF16)) + b_ref[0]


def _ada(c, w_ada, b_ada):
    depth, d, n = w_ada.shape
    bsz = c.shape[0]
    rows = -(-bsz // SUBLANES) * SUBLANES
    c_pad = jnp.pad(c, ((0, rows - bsz), (0, 0)))
    tn = _pick(n, 1024)
    out = pl.pallas_call(
        _ada_body,
        grid=(depth, n // tn),
        in_specs=[
            pl.BlockSpec((rows, d), lambda l, j: (0, 0)),
            pl.BlockSpec((1, d, tn), lambda l, j: (l, 0, j)),
            pl.BlockSpec((1, 1, tn), lambda l, j: (l, 0, j)),
        ],
        out_specs=pl.BlockSpec((1, rows, tn), lambda l, j: (l, 0, j)),
        out_shape=jax.ShapeDtypeStruct((depth, rows, n), F32),
        compiler_params=_params(("parallel", "parallel")),
        name="ada",
    )(c_pad, w_ada, b_ada.reshape(depth, 1, n))
    return out[:, :bsz].reshape(depth, bsz, N_ADA, d)


def _ffn_body(x_ref, ada_ref, nw_ref, w1_ref, w3_ref, w2_ref, fw_ref, o_ref, h_ref,
              *, ada_base, final):
    j = pl.program_id(1)

    @pl.when(j == 0)
    def _():
        shift = ada_ref[0, ada_base:ada_base + 1, :]
        scale = ada_ref[0, ada_base + 1:ada_base + 2, :]
        h_ref[...] = _rms_mod(x_ref[...], nw_ref[...], shift, scale).astype(BF16)
        o_ref[...] = jnp.zeros(o_ref.shape, F32)

    h = h_ref[...]
    a = _dot(h, w1_ref[...])
    g = _dot(h, w3_ref[...])
    u = (_silu(g) * a).astype(BF16)
    o_ref[...] += _dot(u, w2_ref[...])

    @pl.when(j == pl.num_programs(1) - 1)
    def _():
        gate = ada_ref[0, ada_base + 2:ada_base + 3, :]
        y = x_ref[...] + 0.5 * gate * o_ref[...]
        if final:
            y = y * lax.rsqrt(jnp.mean(y * y, axis=-1, keepdims=True) + EPS) * fw_ref[...]
        o_ref[...] = y


def _ffn(x2, ada, ada_base, nw, w13, w2, layer, fw, seq, final):
    m, d = x2.shape
    tf = w13.shape[3]
    nf = w13.shape[1] // 2
    assert w2.shape[2] == nf * tf and w2.shape[3] == d
    tm = _pick(seq, 1024)
    per_batch = seq // tm
    return pl.pallas_call(
        functools.partial(_ffn_body, ada_base=ada_base, final=final),
        grid=(m // tm, nf),
        in_specs=[
            pl.BlockSpec((tm, d), lambda i, j: (i, 0), pipeline_mode=pl.Buffered(1)),
            pl.BlockSpec((1, N_ADA, d), lambda i, j: (i // per_batch, 0, 0)),
            pl.BlockSpec((1, d), lambda i, j: (0, 0)),
            pl.BlockSpec((None, None, d, tf), lambda i, j: (layer, j, 0, 0)),
            pl.BlockSpec((None, None, d, tf), lambda i, j: (layer, j + nf, 0, 0)),
            pl.BlockSpec((None, None, tf, d), lambda i, j: (layer, 0, j, 0)),
            pl.BlockSpec((1, d), lambda i, j: (0, 0)),
        ],
        out_specs=pl.BlockSpec((tm, d), lambda i, j: (i, 0)),
        out_shape=jax.ShapeDtypeStruct((m, d), F32),
        scratch_shapes=[pltpu.VMEM((tm, d), BF16)],
        compiler_params=_params(("parallel", "arbitrary")),
        name="ffn",
    )(x2, ada, nw.reshape(1, d), w13, w13, w2, fw.reshape(1, d))


def _halo_rows(taps):
    return -(-(taps - 1) // SUBLANES) * SUBLANES


def _shift_residues(taps):
    base = _halo_rows(taps) - (taps - 1)
    return sorted({(base + k) % SUBLANES for k in range(taps)} - {0})


def _conv_lane_block(ext_ref, sh_ref, w_ref, b_ref, cols, taps, rows, rb, emit):
    halo = _halo_rows(taps)
    base = halo - (taps - 1)
    span = halo + rows - SUBLANES
    slot = {s: i for i, s in enumerate(_shift_residues(taps))}
    for s, i in slot.items():
        sh_ref[i, 0:span, :] = ext_ref[s:s + span, cols]
    for r0 in range(0, rows, rb):
        acc = None
        for k in range(taps):
            s = (base + k) % SUBLANES
            a = base + k - s + r0
            win = ext_ref[a:a + rb, cols] if s == 0 else sh_ref[slot[s], a:a + rb, :]
            term = w_ref[k:k + 1, cols] * win
            acc = term if acc is None else acc + term
        emit(r0, acc + b_ref[:, cols])


def _inproj_body(x_ref, ada_ref, nw_ref, w_ref, wdt_ref, dtb_ref, cw_ref, cb_ref,
                 zx_ref, gate_ref, u_ref, dt_ref, h_ref, ext_ref, sh_ref, carry_ref,
                 *, ada_base, n_z, n_x, n_gate, n_glu, per_batch, taps, rb, lb):
    i = pl.program_id(0)
    j = pl.program_id(1)
    n_zx = n_z + n_x
    n_main = n_zx + n_gate + n_glu
    tm, tn = zx_ref.shape
    halo = _halo_rows(taps)

    @pl.when(j == 0)
    def _():
        shift = ada_ref[0, ada_base:ada_base + 1, :]
        scale = ada_ref[0, ada_base + 1:ada_base + 2, :]
        h_ref[...] = _rms_mod(x_ref[...], nw_ref[...], shift, scale).astype(BF16)

    @pl.when(j < n_z)
    def _():
        zx_ref[...] = _dot(h_ref[...], w_ref[...])

    @pl.when(jnp.logical_and(j >= n_z, j < n_zx))
    def _():
        xt = j - n_z

        @pl.when(i % per_batch == 0)
        def _():
            ext_ref[0:halo, :] = jnp.zeros((halo, tn), F32)

        @pl.when(i % per_batch != 0)
        def _():
            ext_ref[0:halo, :] = carry_ref[xt]

        ext_ref[halo:halo + tm, :] = _dot(h_ref[...], w_ref[...])
        carry_ref[xt] = ext_ref[tm:tm + halo, :]
        for c0 in range(0, tn, lb):
            cols = slice(c0, c0 + lb)

            def emit(r0, acc, cols=cols):
                zx_ref[r0:r0 + rb, cols] = _silu(acc)

            _conv_lane_block(ext_ref, sh_ref, cw_ref, cb_ref, cols, taps, tm, rb, emit)

    @pl.when(jnp.logical_and(j >= n_zx, j < n_zx + n_gate))
    def _():
        gate_ref[...] = jax.nn.sigmoid(_dot(h_ref[...], w_ref[...]))

    @pl.when(jnp.logical_and(j >= n_zx + n_gate, j < n_main))
    def _():
        acc = _dot(h_ref[...], w_ref[...])
        half = acc.shape[1] // 2
        u_ref[...] = acc[:, :half] * jax.nn.sigmoid(acc[:, half:])

    @pl.when(j == n_main)
    def _():
        dt_ref[...] = _softplus(_dot(h_ref[...], wdt_ref[...]) + dtb_ref[...])


def _inproj(x2, ada, ada_base, nw, w_main, w_dt, dt_b, conv_w, conv_b, widths, seq, tn):
    m, d = x2.shape
    z_w, gate_w, glu_w = widths
    taps, x_w = conv_w.shape
    n_z, n_x, n_gate, n_glu = z_w // tn, x_w // tn, gate_w // tn, 2 * glu_w // tn
    n_zx = n_z + n_x
    n_main = n_zx + n_gate + n_glu
    tm = _pick(seq, 1024)
    per_batch = seq // tm
    dtw = w_dt.shape[1]
    halo = _halo_rows(taps)
    lb = _pick(tn, 256)
    body = functools.partial(_inproj_body, ada_base=ada_base, n_z=n_z, n_x=n_x, n_gate=n_gate,
                             n_glu=n_glu, per_batch=per_batch, taps=taps, rb=_pick(tm, 64), lb=lb)
    conv_tile = lambda i, j: (0, jnp.clip(j - n_z, 0, n_x - 1))
    return pl.pallas_call(
        body,
        grid=(m // tm, n_main + 1),
        in_specs=[
            pl.BlockSpec((tm, d), lambda i, j: (i, 0), pipeline_mode=pl.Buffered(1)),
            pl.BlockSpec((1, N_ADA, d), lambda i, j: (i // per_batch, 0, 0)),
            pl.BlockSpec((1, d), lambda i, j: (0, 0)),
            pl.BlockSpec((None, d, tn), lambda i, j: (jnp.minimum(j, n_main - 1), 0, 0)),
            pl.BlockSpec((d, dtw), lambda i, j: (0, 0)),
            pl.BlockSpec((1, dtw), lambda i, j: (0, 0)),
            pl.BlockSpec((taps, tn), conv_tile),
            pl.BlockSpec((1, tn), conv_tile),
        ],
        out_specs=[
            pl.BlockSpec((tm, tn), lambda i, j: (i, jnp.minimum(j, n_zx - 1))),
            pl.BlockSpec((tm, tn), lambda i, j: (i, jnp.clip(j - n_zx, 0, n_gate - 1))),
            pl.BlockSpec((tm, tn // 2), lambda i, j: (i, jnp.clip(j - n_zx - n_gate, 0, n_glu - 1))),
            pl.BlockSpec((tm, dtw), lambda i, j: (i, 0)),
        ],
        out_shape=[
            jax.ShapeDtypeStruct((m, z_w + x_w), F32),
            jax.ShapeDtypeStruct((m, gate_w), F32),
            jax.ShapeDtypeStruct((m, glu_w), F32),
            jax.ShapeDtypeStruct((m, dtw), F32),
        ],
        scratch_shapes=[pltpu.VMEM((tm, d), BF16),
                        pltpu.VMEM((halo + tm, tn), F32),
                        pltpu.VMEM((len(_shift_residues(taps)), halo + tm, lb), F32),
                        pltpu.VMEM((n_x, halo, tn), F32)],
        compiler_params=_params(("arbitrary", "arbitrary")),
        name="inproj",
    )(x2, ada, nw.reshape(1, d), w_main, w_dt, dt_b, conv_w, conv_b.reshape(1, x_w))


def _conf_conv_body(u_ref, w_ref, b_ref, g_ref, beta_ref, o_ref, ext_ref, sh_ref, y_ref,
                    *, taps, rb, cb):
    rows, width = u_ref.shape[1], u_ref.shape[2]
    halo = _halo_rows(taps)
    first = pl.program_id(1) == 0

    @pl.when(first)
    def _():
        ext_ref[0:halo, :] = jnp.zeros((halo, width), F32)

    @pl.when(jnp.logical_not(first))
    def _():
        ext_ref[0:halo, :] = ext_ref[rows:rows + halo, :]

    ext_ref[halo:halo + rows, :] = u_ref[0]

    def lane_block(ci, carry):
        cols = pl.ds(pl.multiple_of(ci * cb, cb), cb)

        def emit(r0, acc):
            y_ref[r0:r0 + rb, cols] = acc

        _conv_lane_block(ext_ref, sh_ref, w_ref, b_ref, cols, taps, rows, rb, emit)
        return carry

    lax.fori_loop(0, width // cb, lane_block, 0)
    y = y_ref[...]
    mu = jnp.mean(y, axis=-1, keepdims=True)
    yc = y - mu
    var = jnp.mean(yc * yc, axis=-1, keepdims=True)
    yn = yc * lax.rsqrt(var + EPS) * g_ref[...] + beta_ref[...]
    o_ref[0] = _silu(yn).astype(BF16)


def _conf_conv(u3, dw_w, dw_b, ln_g, ln_b):
    bsz, seq, c = u3.shape
    taps = dw_w.shape[0]
    tr = _pick(seq, 256)
    cb = _pick(c, 256)
    halo = _halo_rows(taps)
    vec = lambda v: v.reshape(1, c)
    row_spec = pl.BlockSpec((1, c), lambda b, t: (0, 0))
    return pl.pallas_call(
        functools.partial(_conf_conv_body, taps=taps, rb=_pick(tr, 64), cb=cb),
        grid=(bsz, seq // tr),
        in_specs=[
            pl.BlockSpec((1, tr, c), lambda b, t: (b, t, 0)),
            pl.BlockSpec((taps, c), lambda b, t: (0, 0)),
            row_spec, row_spec, row_spec,
        ],
        out_specs=pl.BlockSpec((1, tr, c), lambda b, t: (b, t, 0)),
        out_shape=jax.ShapeDtypeStruct((bsz, seq, c), BF16),
        scratch_shapes=[pltpu.VMEM((tr + halo, c), F32),
                        pltpu.VMEM((len(_shift_residues(taps)), tr + halo, cb), F32),
                        pltpu.VMEM((tr, c), F32)],
        compiler_params=_params(("parallel", "arbitrary")),
        name="conf_conv",
    )(u3, dw_w, vec(dw_b), vec(ln_g), vec(ln_b))


def _split3(v):
    hi = v.astype(BF16)
    r1 = v - hi.astype(F32)
    mid = r1.astype(BF16)
    lo = (r1 - mid.astype(F32)).astype(BF16)
    return hi, mid, lo


def _pack_split(v, lo_half):
    hi = v.astype(BF16).astype(F32)
    r1 = v - hi
    mid = r1.astype(BF16).astype(F32)
    lo = r1 - mid
    first = jnp.where(lo_half, hi, pltpu.roll(mid, HEAD_DIM, 1))
    return jnp.concatenate([first.astype(BF16), lo.astype(BF16)], axis=1)


def _ssd_body(xs_ref, b_ref, c_ref, z_ref, dt_ref, alog_ref, dsk_ref, nw_ref, e_ref,
              yg_ref, ss_ref, s_ref, acsp_ref, dtp_ref, acst_ref, *, heads_per_group):
    t = pl.program_id(1)
    g = pl.program_id(2)
    q = SSD_CHUNK
    rows = xs_ref.shape[1]
    n_chunks = rows // q
    n_pairs = xs_ref.shape[2] // LANES
    causal = lax.broadcasted_iota(jnp.int32, (q, q), 0) >= lax.broadcasted_iota(jnp.int32, (q, q), 1)
    lane = lax.broadcasted_iota(jnp.int32, (q, LANES), 1)
    lo_half = lane < HEAD_DIM

    @pl.when(t == 0)
    def _():
        s_ref[g] = jnp.zeros(s_ref.shape[1:], F32)

    @pl.when(g == 0)
    def _():
        ss_ref[...] = jnp.zeros(ss_ref.shape, F32)
        tri = causal.astype(BF16)
        a2 = -jnp.exp(alog_ref[...]) * LOG2E
        for ci in range(n_chunks):
            rs = slice(ci * q, (ci + 1) * q)
            hi, mid, lo = _split3(dt_ref[0, rs, :] * a2)
            cs = _dot(tri, jnp.concatenate([hi, mid, lo], axis=1))
            acs = (cs[:, :LANES] + cs[:, LANES:2 * LANES]) + cs[:, 2 * LANES:]
            acsp_ref[ci] = _pack_split(acs, lo_half)
            dtp_ref[ci] = _pack_split(dt_ref[0, rs, :], lo_half)
            acst_ref[ci] = acs.T

    head_row0 = pl.multiple_of(g * heads_per_group, SUBLANES)
    n_bc = heads_per_group * LANES

    pre = []
    for ci in range(n_chunks):
        rs = slice(ci * q, (ci + 1) * q)
        spread = _dot(acsp_ref[ci], e_ref[...])
        dt_pairs = _dot(dtp_ref[ci], e_ref[:, n_bc:])
        acs_t = acst_ref[ci, pl.ds(head_row0, SUBLANES), :]
        bg = b_ref[0, rs, :].astype(BF16)
        cg = c_ref[0, rs, :].astype(BF16)
        cb = lax.dot_general(cg, bg, (((1,), (1,)), ((), ())), preferred_element_type=F32)
        y_diag, off_scale, xw_parts, decay_parts = [], [], [], []
        for p in range(n_pairs):
            ls = slice(p * LANES, (p + 1) * LANES)
            h0, h1 = 2 * p, 2 * p + 1
            bc0 = spread[:, h0 * LANES:(h0 + 1) * LANES]
            bc1 = spread[:, h1 * LANES:(h1 + 1) * LANES]
            m0 = cb * jnp.exp2(jnp.where(causal, bc0 - acs_t[h0:h0 + 1, :], -jnp.inf))
            m1 = cb * jnp.exp2(jnp.where(causal, bc1 - acs_t[h1:h1 + 1, :], -jnp.inf))
            acs_pair = spread[:, n_bc + p * LANES:n_bc + (p + 1) * LANES]
            xdt = xs_ref[0, rs, ls] * dt_pairs[:, ls]
            lhs = jnp.concatenate([m0.astype(BF16), m1.astype(BF16)], axis=1)
            rhs = jnp.concatenate([jnp.where(lo_half, xdt, 0.0).astype(BF16),
                                   jnp.where(lo_half, 0.0, xdt).astype(BF16)], axis=0)
            y_diag.append(_dot(lhs, rhs))
            off_scale.append(jnp.exp2(acs_pair))
            last_pair = acs_pair[q - 1:q, :]
            xw_parts.append((xdt * jnp.exp2(last_pair - acs_pair)).astype(BF16))
            decay_parts.append(jnp.exp2(last_pair))
        xw = jnp.concatenate(xw_parts, axis=1)
        new = lax.dot_general(bg, xw, (((0,), (0,)), ((), ())), preferred_element_type=F32)
        pre.append((cg, y_diag, off_scale, jnp.concatenate(decay_parts, axis=1), new))

    state = s_ref[g]
    for ci in range(n_chunks):
        rs = slice(ci * q, (ci + 1) * q)
        cg, y_diag, off_scale, decay, new = pre[ci]
        y_off = _dot(cg, state.astype(BF16))
        state = state * decay + new
        sq = jnp.zeros((q, LANES), F32)
        for p in range(n_pairs):
            ls = slice(p * LANES, (p + 1) * LANES)
            y = y_diag[p] + y_off[:, ls] * off_scale[p] + dsk_ref[0, :, ls] * xs_ref[0, rs, ls]
            v = y * _silu(z_ref[0, rs, ls])
            sq = sq + v * v
            yg_ref[0, rs, ls] = (v * nw_ref[:, ls]).astype(BF16)
        ss_ref[0, rs, :] += sq
    s_ref[g] = state


def _head_spread_matrices(groups, hpg):
    r = jnp.arange(2 * LANES)
    head_of_row = jnp.where(r < 3 * HEAD_DIM, r % HEAD_DIM, -1)
    n = jnp.arange(hpg * LANES + hpg * HEAD_DIM)
    local_head = jnp.where(n < hpg * LANES, n // LANES, (n - hpg * LANES) // HEAD_DIM)
    g = jnp.arange(groups)
    want = g[:, None, None] * hpg + local_head[None, None, :]
    return (head_of_row[None, :, None] == want).astype(BF16)


def _ssd(zx3, dt3, a_pad, dsk, nw, d_inner):
    bsz, seq, width = zx3.shape
    groups = SSM_GROUPS
    gw = d_inner // groups
    n_state = (width - 2 * d_inner) // (2 * groups)
    hpg = gw // HEAD_DIM
    assert gw % LANES == 0 and d_inner % n_state == 0 and hpg == SUBLANES
    assert dt3.shape[2] == LANES and groups * hpg <= HEAD_DIM
    spread = _head_spread_matrices(groups, hpg)
    tr = _pick(seq, 512)
    assert tr % SSD_CHUNK == 0
    x_off = d_inner // gw
    b_off = 2 * d_inner // n_state
    return pl.pallas_call(
        functools.partial(_ssd_body, heads_per_group=hpg),
        grid=(bsz, seq // tr, groups),
        in_specs=[
            pl.BlockSpec((1, tr, gw), lambda b, t, g: (b, t, x_off + g)),
            pl.BlockSpec((1, tr, n_state), lambda b, t, g: (b, t, b_off + g)),
            pl.BlockSpec((1, tr, n_state), lambda b, t, g: (b, t, b_off + groups + g)),
            pl.BlockSpec((1, tr, gw), lambda b, t, g: (b, t, g)),
            pl.BlockSpec((1, tr, LANES), lambda b, t, g: (b, t, 0)),
            pl.BlockSpec((1, LANES), lambda b, t, g: (0, 0)),
            pl.BlockSpec((1, 1, gw), lambda b, t, g: (g, 0, 0)),
            pl.BlockSpec((1, gw), lambda b, t, g: (0, g)),
            pl.BlockSpec((None,) + spread.shape[1:], lambda b, t, g: (g, 0, 0)),
        ],
        out_specs=[
            pl.BlockSpec((1, tr, gw), lambda b, t, g: (b, t, g)),
            pl.BlockSpec((1, tr, LANES), lambda b, t, g: (b, t, 0)),
        ],
        out_shape=[
            jax.ShapeDtypeStruct((bsz, seq, d_inner), BF16),
            jax.ShapeDtypeStruct((bsz, seq, LANES), F32),
        ],
        scratch_shapes=[pltpu.VMEM((groups, n_state, gw), F32),
                        pltpu.VMEM((tr // SSD_CHUNK, SSD_CHUNK, 2 * LANES), BF16),
                        pltpu.VMEM((tr // SSD_CHUNK, SSD_CHUNK, 2 * LANES), BF16),
                        pltpu.VMEM((tr // SSD_CHUNK, LANES, SSD_CHUNK), F32)],
        compiler_params=_params(("parallel", "arbitrary", "arbitrary")),
        name="ssd",
    )(zx3, zx3, zx3, zx3, dt3, a_pad, dsk, nw.reshape(1, d_inner), spread)


def _merge_body(yg_ref, ss_ref, uc_ref, gs_ref, gc_ref, x_ref, ada_ref, wso_ref, wpw_ref,
                bpw_ref, wo_ref, o_ref, *, ada_base, d_inner):
    j = pl.program_id(1)

    @pl.when(j == 0)
    def _():
        o_ref[...] = jnp.zeros(o_ref.shape, F32)

    mean_sq = jnp.sum(ss_ref[...], axis=-1, keepdims=True) * (1.0 / d_inner)
    y_ssm = _dot(yg_ref[...], wso_ref[...]) * lax.rsqrt(mean_sq + EPS)
    y_conv = _dot(uc_ref[...], wpw_ref[...]) + bpw_ref[...]
    merged = (gs_ref[...] * y_ssm + gc_ref[...] * y_conv).astype(BF16)
    o_ref[...] += _dot(merged, wo_ref[...])

    @pl.when(j == pl.num_programs(1) - 1)
    def _():
        o_ref[...] = x_ref[...] + ada_ref[0, ada_base + 2:ada_base + 3, :] * o_ref[...]


def _merge(yg2, ss2, uc2, gates2, x2, ada, ada_base, wso, wpw, bpw, wo, layer, seq):
    m, d = x2.shape
    d_inner = yg2.shape[1]
    c = uc2.shape[1]
    tm = _pick(seq, 512)
    tn = wso.shape[3]
    nn = d // tn
    per_batch = seq // tm
    return pl.pallas_call(
        functools.partial(_merge_body, ada_base=ada_base, d_inner=d_inner),
        grid=(m // tm, nn),
        in_specs=[
            pl.BlockSpec((tm, d_inner), lambda i, j: (i, 0)),
            pl.BlockSpec((tm, LANES), lambda i, j: (i, 0)),
            pl.BlockSpec((tm, c), lambda i, j: (i, 0)),
            pl.BlockSpec((tm, tn), lambda i, j: (i, j)),
            pl.BlockSpec((tm, tn), lambda i, j: (i, j + nn)),
            pl.BlockSpec((tm, d), lambda i, j: (i, 0)),
            pl.BlockSpec((1, N_ADA, d), lambda i, j: (i // per_batch, 0, 0)),
            pl.BlockSpec((None, None, d_inner, tn), lambda i, j: (layer, j, 0, 0)),
            pl.BlockSpec((None, None, c, tn), lambda i, j: (layer, j, 0, 0)),
            pl.BlockSpec((1, tn), lambda i, j: (0, j)),
            pl.BlockSpec((None, None, tn, d), lambda i, j: (layer, 0, j, 0)),
        ],
        out_specs=pl.BlockSpec((tm, d), lambda i, j: (i, 0)),
        out_shape=jax.ShapeDtypeStruct((m, d), F32),
        compiler_params=_params(("parallel", "arbitrary")),
        name="merge",
    )(yg2, ss2, uc2, gates2, gates2, x2, ada, wso, wpw, bpw.reshape(1, d), wo)


def _pad_lanes(v):
    return jnp.pad(v, [(0, 0)] * (v.ndim - 1) + [(0, LANES - v.shape[-1])])


def _interleave_tiles(a, b, half):
    d, n = a.shape
    return jnp.stack([a.reshape(d, n // half, half), b.reshape(d, n // half, half)],
                     axis=2).reshape(d, 2 * n)


def kernel(x, c, w_ada, b_ada, norm_ffn1, ffn1_w13, ffn1_w2, norm_mix, w_in, ssm_conv_w, ssm_conv_b, dt_bias, a_log, d_skip, ssm_norm_w, w_ssm_out, dw_w, dw_b, conv_ln_g, conv_ln_b, w_pw2, b_pw2, w_o, norm_ffn2, ffn2_w13, ffn2_w2, final_norm):
    bsz, seq, d = x.shape
    depth = w_ada.shape[0]
    groups = SSM_GROUPS
    d_inner = w_ssm_out.shape[1]
    conv_dim = ssm_conv_w.shape[2]
    heads = dt_bias.shape[1]
    conf_c = dw_w.shape[2]
    assert d_inner // heads == HEAD_DIM
    p_dt = d_inner + conv_dim
    p_glu = p_dt + heads
    p_gate = p_glu + 2 * conf_c
    gate_w = w_in.shape[2] - p_gate
    tn = _pick(math.gcd(d_inner, conv_dim, gate_w, 2 * conf_c), 1024)

    ada_all = _ada(c, w_ada, b_ada)
    tf = _pick(ffn1_w2.shape[1], 512)
    w13_1, w2_1 = _to_bf16(ffn1_w13, tf), _to_bf16(ffn1_w2, d)
    w13_2, w2_2 = _to_bf16(ffn2_w13, tf), _to_bf16(ffn2_w2, d)
    tmn = _pick(d, 512)
    wso, wpw, wo = _to_bf16(w_ssm_out, tmn), _to_bf16(w_pw2, tmn), _to_bf16(w_o, d)
    x2 = x.reshape(bsz * seq, d)
    for l in range(depth):
        ada = ada_all[l]
        wl = w_in[l]
        w_glu = _interleave_tiles(wl[:, p_glu:p_glu + conf_c], wl[:, p_glu + conf_c:p_gate], tn // 2)
        w_main = jnp.concatenate([wl[:, :p_dt], wl[:, p_gate:], w_glu], axis=1).astype(BF16)
        w_main = w_main.reshape(d, -1, tn).transpose(1, 0, 2)
        w_dt = _pad_lanes(wl[:, p_dt:p_glu]).astype(BF16)
        dt_b = _pad_lanes(dt_bias[l]).reshape(1, LANES)
        a_pad = _pad_lanes(a_log[l]).reshape(1, LANES)
        dsk = jnp.repeat(d_skip[l], HEAD_DIM).reshape(groups, 1, d_inner // groups)

        x2 = _ffn(x2, ada, 0, norm_ffn1[l], w13_1, w2_1, l, final_norm, seq, False)

        zx, gates, u, dt = _inproj(x2, ada, 3, norm_mix[l], w_main, w_dt, dt_b,
                                   ssm_conv_w[l], ssm_conv_b[l], (d_inner, gate_w, conf_c), seq, tn)

        yg, ss = _ssd(zx.reshape(bsz, seq, p_dt), dt.reshape(bsz, seq, LANES), a_pad, dsk,
                      ssm_norm_w[l], d_inner)
        uc = _conf_conv(u.reshape(bsz, seq, conf_c), dw_w[l], dw_b[l], conv_ln_g[l], conv_ln_b[l])

        x2 = _merge(yg.reshape(bsz * seq, d_inner), ss.reshape(bsz * seq, LANES),
                    uc.reshape(bsz * seq, conf_c), gates, x2, ada, 3,
                    wso, wpw, b_pw2[l], wo, l, seq)

        x2 = _ffn(x2, ada, 6, norm_ffn2[l], w13_2, w2_2, l, final_norm, seq, l == depth - 1)
    return x2.reshape(bsz, seq, d)
```

```python
import functools
import math

import jax
import jax.numpy as jnp
from jax import lax
from jax.experimental import pallas as pl
from jax.experimental.pallas import tpu as pltpu

F32 = jnp.float32
BF16 = jnp.bfloat16
EPS = 1e-6
LOG2E = 1.4426950408889634

SSM_GROUPS = 8
N_ADA = 9

LANES = 128
SUBLANES = 8
VMEM_LIMIT_BYTES = 60 * 1024 * 1024

SSD_CHUNK = 128
HEAD_DIM = 64
CAST_BLOCK_BYTES = 8 * 1024 * 1024


def _params(semantics):
    return pltpu.CompilerParams(dimension_semantics=semantics,
                                vmem_limit_bytes=VMEM_LIMIT_BYTES)


def _pick(total, want):
    t = min(total, want)
    while total % t:
        t //= 2
    return t


def _silu(v):
    return v * jax.nn.sigmoid(v)


def _softplus(v):
    return jnp.maximum(v, 0.0) + jnp.log1p(jnp.exp(-jnp.abs(v)))


def _rms_mod(x, nw, shift, scale):
    y = x * lax.rsqrt(jnp.mean(x * x, axis=-1, keepdims=True) + EPS) * nw
    return y * (1.0 + scale) + shift


def _dot(a, b):
    return jnp.dot(a, b, preferred_element_type=F32)


def _cast_body(w_ref, o_ref):
    tc = o_ref.shape[-1]
    for t in range(o_ref.shape[0]):
        o_ref[t] = w_ref[:, t * tc:(t + 1) * tc].astype(BF16)


def _to_bf16(w, tc):
    depth, rows, cols = w.shape
    want = 1 << int(math.log2(max(SUBLANES, CAST_BLOCK_BYTES // (4 * cols))))
    tr = _pick(rows, want)
    nt = cols // tc
    per_layer = rows // tr
    return pl.pallas_call(
        _cast_body,
        grid=(depth * per_layer,),
        in_specs=[pl.BlockSpec((None, tr, cols), lambda i: (i // per_layer, i % per_layer, 0))],
        out_specs=pl.BlockSpec((None, nt, tr, tc), lambda i: (i // per_layer, 0, i % per_layer, 0)),
        out_shape=jax.ShapeDtypeStruct((depth, nt, rows, tc), BF16),
        compiler_params=_params(("parallel",)),
        name="cast",
    )(w)


def _repack_body(w_ref, o_ref, odt_ref, *, p_dt, heads, conf_c, gate_w, half):
    p_glu = p_dt + heads
    p_gate = p_glu + 2 * conf_c
    o_ref[:, 0:p_dt] = w_ref[:, 0:p_dt].astype(BF16)
    o_ref[:, p_dt:p_dt + gate_w] = w_ref[:, p_gate:p_gate + gate_w].astype(BF16)
    for i in range(conf_c // half):
        dst = p_dt + gate_w + 2 * i * half
        o_ref[:, dst:dst + half] = w_ref[:, p_glu + i * half:p_glu + (i + 1) * half].astype(BF16)
        o_ref[:, dst + half:dst + 2 * half] = (
            w_ref[:, p_glu + conf_c + i * half:p_glu + conf_c + (i + 1) * half].astype(BF16))
    odt_ref[:, 0:heads] = w_ref[:, p_dt:p_glu].astype(BF16)
    odt_ref[:, heads:] = jnp.zeros((odt_ref.shape[0], odt_ref.shape[1] - heads), BF16)


def _repack_w_in(w_in, p_dt, heads, conf_c, gate_w, half):
    depth, d, cols = w_in.shape
    tr = _pick(d, 128)
    per_layer = d // tr
    n_main = cols - heads
    return pl.pallas_call(
        functools.partial(_repack_body, p_dt=p_dt, heads=heads, conf_c=conf_c, gate_w=gate_w, half=half),
        grid=(depth * per_layer,),
        in_specs=[pl.BlockSpec((None, tr, cols), lambda i: (i // per_layer, i % per_layer, 0))],
        out_specs=[pl.BlockSpec((None, tr, n_main), lambda i: (i // per_layer, i % per_layer, 0)),
                   pl.BlockSpec((None, tr, LANES), lambda i: (i // per_layer, i % per_layer, 0))],
        out_shape=[jax.ShapeDtypeStruct((depth, d, n_main), BF16),
                   jax.ShapeDtypeStruct((depth, d, LANES), BF16)],
        compiler_params=_params(("parallel",)),
        name="repack",
    )(w_in)


def _ada_body(c_ref, w_ref, b_ref, o_ref):
    c = c_ref[...]
    o_ref[0] = _dot(_silu(c).astype(BF16), w_ref[0].astype(BF16)) + b_ref[0]


def _ada(c, w_ada, b_ada):
    depth, d, n = w_ada.shape
    bsz = c.shape[0]
    rows = -(-bsz // SUBLANES) * SUBLANES
    c_pad = jnp.pad(c, ((0, rows - bsz), (0, 0)))
    tn = _pick(n, 1024)
    out = pl.pallas_call(
        _ada_body,
        grid=(depth, n // tn),
        in_specs=[
            pl.BlockSpec((rows, d), lambda l, j: (0, 0)),
            pl.BlockSpec((1, d, tn), lambda l, j: (l, 0, j)),
            pl.BlockSpec((1, 1, tn), lambda l, j: (l, 0, j)),
        ],
        out_specs=pl.BlockSpec((1, rows, tn), lambda l, j: (l, 0, j)),
        out_shape=jax.ShapeDtypeStruct((depth, rows, n), F32),
        compiler_params=_params(("parallel", "parallel")),
        name="ada",
    )(c_pad, w_ada, b_ada.reshape(depth, 1, n))
    return out[:, :bsz].reshape(depth, bsz, N_ADA, d)


def _ffn_body(x_ref, ada_ref, nw_ref, w1_ref, w3_ref, w2_ref, fw_ref, o_ref, h_ref,
              *, ada_base, final):
    j = pl.program_id(1)

    @pl.when(j == 0)
    def _():
        shift = ada_ref[0, ada_base:ada_base + 1, :]
        scale = ada_ref[0, ada_base + 1:ada_base + 2, :]
        h_ref[...] = _rms_mod(x_ref[...], nw_ref[...], shift, scale).astype(BF16)
        o_ref[...] = jnp.zeros(o_ref.shape, F32)

    h = h_ref[...]
    a = _dot(h, w1_ref[...])
    g = _dot(h, w3_ref[...])
    u = (_silu(g) * a).astype(BF16)
    o_ref[...] += _dot(u, w2_ref[...])

    @pl.when(j == pl.num_programs(1) - 1)
    def _():
        gate = ada_ref[0, ada_base + 2:ada_base + 3, :]
        y = x_ref[...] + 0.5 * gate * o_ref[...]
        if final:
            y = y * lax.rsqrt(jnp.mean(y * y, axis=-1, keepdims=True) + EPS) * fw_ref[...]
        o_ref[...] = y


def _ffn(x2, ada, ada_base, nw, w13, w2, layer, fw, seq, final):
    m, d = x2.shape
    tf = w13.shape[3]
    nf = w13.shape[1] // 2
    assert w2.shape[2] == nf * tf and w2.shape[3] == d
    tm = _pick(seq, 1024)
    per_batch = seq // tm
    return pl.pallas_call(
        functools.partial(_ffn_body, ada_base=ada_base, final=final),
        grid=(m // tm, nf),
        in_specs=[
            pl.BlockSpec((tm, d), lambda i, j: (i, 0), pipeline_mode=pl.Buffered(1)),
            pl.BlockSpec((1, N_ADA, d), lambda i, j: (i // per_batch, 0, 0)),
            pl.BlockSpec((1, d), lambda i, j: (0, 0)),
            pl.BlockSpec((None, None, d, tf), lambda i, j: (layer, j, 0, 0)),
            pl.BlockSpec((None, None, d, tf), lambda i, j: (layer, j + nf, 0, 0)),
            pl.BlockSpec((None, None, tf, d), lambda i, j: (layer, 0, j, 0)),
            pl.BlockSpec((1, d), lambda i, j: (0, 0)),
        ],
        out_specs=pl.BlockSpec((tm, d), lambda i, j: (i, 0)),
        out_shape=jax.ShapeDtypeStruct((m, d), F32),
        scratch_shapes=[pltpu.VMEM((tm, d), BF16)],
        compiler_params=_params(("parallel", "arbitrary")),
        name="ffn",
    )(x2, ada, nw.reshape(1, d), w13, w13, w2, fw.reshape(1, d))


def _halo_rows(taps):
    return -(-(taps - 1) // SUBLANES) * SUBLANES


def _shift_residues(taps):
    base = _halo_rows(taps) - (taps - 1)
    return sorted({(base + k) % SUBLANES for k in range(taps)} - {0})


def _conv_lane_block(ext_ref, sh_ref, w_ref, b_ref, cols, taps, rows, rb, emit):
    halo = _halo_rows(taps)
    base = halo - (taps - 1)
    span = halo + rows - SUBLANES
    slot = {s: i for i, s in enumerate(_shift_residues(taps))}
    for s, i in slot.items():
        sh_ref[i, 0:span, :] = ext_ref[s:s + span, cols]
    for r0 in range(0, rows, rb):
        acc = None
        for k in range(taps):
            s = (base + k) % SUBLANES
            a = base + k - s + r0
            win = ext_ref[a:a + rb, cols] if s == 0 else sh_ref[slot[s], a:a + rb, :]
            term = w_ref[k:k + 1, cols] * win
            acc = term if acc is None else acc + term
        emit(r0, acc + b_ref[:, cols])


def _inproj_body(x_ref, ada_ref, nw_ref, w_ref, wdt_ref, dtb_ref, cw_ref, cb_ref,
                 zx_ref, gate_ref, u_ref, dt_ref, h_ref, ext_ref, sh_ref, carry_ref,
                 *, ada_base, n_z, n_x, n_gate, n_glu, per_batch, taps, rb, lb):
    i = pl.program_id(0)
    j = pl.program_id(1)
    n_zx = n_z + n_x
    n_main = n_zx + n_gate + n_glu
    tm, tn = zx_ref.shape
    halo = _halo_rows(taps)

    @pl.when(j == 0)
    def _():
        shift = ada_ref[0, ada_base:ada_base + 1, :]
        scale = ada_ref[0, ada_base + 1:ada_base + 2, :]
        h_ref[...] = _rms_mod(x_ref[...], nw_ref[...], shift, scale).astype(BF16)

    @pl.when(j < n_z)
    def _():
        zx_ref[...] = _dot(h_ref[...], w_ref[...])

    @pl.when(jnp.logical_and(j >= n_z, j < n_zx))
    def _():
        xt = j - n_z

        @pl.when(i % per_batch == 0)
        def _():
            ext_ref[0:halo, :] = jnp.zeros((halo, tn), F32)

        @pl.when(i % per_batch != 0)
        def _():
            ext_ref[0:halo, :] = carry_ref[xt]

        ext_ref[halo:halo + tm, :] = _dot(h_ref[...], w_ref[...])
        carry_ref[xt] = ext_ref[tm:tm + halo, :]
        for c0 in range(0, tn, lb):
            cols = slice(c0, c0 + lb)

            def emit(r0, acc, cols=cols):
                zx_ref[r0:r0 + rb, cols] = _silu(acc)

            _conv_lane_block(ext_ref, sh_ref, cw_ref, cb_ref, cols, taps, tm, rb, emit)

    @pl.when(jnp.logical_and(j >= n_zx, j < n_zx + n_gate))
    def _():
        gate_ref[...] = jax.nn.sigmoid(_dot(h_ref[...], w_ref[...]))

    @pl.when(jnp.logical_and(j >= n_zx + n_gate, j < n_main))
    def _():
        acc = _dot(h_ref[...], w_ref[...])
        half = acc.shape[1] // 2
        u_ref[...] = acc[:, :half] * jax.nn.sigmoid(acc[:, half:])

    @pl.when(j == n_main)
    def _():
        dt_ref[...] = _softplus(_dot(h_ref[...], wdt_ref[...]) + dtb_ref[...])


def _inproj(x2, ada, ada_base, nw, w_main, w_dt, layer, dt_b, conv_w, conv_b, widths, seq, tn):
    m, d = x2.shape
    z_w, gate_w, glu_w = widths
    taps, x_w = conv_w.shape
    n_z, n_x, n_gate, n_glu = z_w // tn, x_w // tn, gate_w // tn, 2 * glu_w // tn
    n_zx = n_z + n_x
    n_main = n_zx + n_gate + n_glu
    tm = _pick(seq, 1024)
    per_batch = seq // tm
    dtw = w_dt.shape[2]
    halo = _halo_rows(taps)
    lb = _pick(tn, 256)
    body = functools.partial(_inproj_body, ada_base=ada_base, n_z=n_z, n_x=n_x, n_gate=n_gate,
                             n_glu=n_glu, per_batch=per_batch, taps=taps, rb=_pick(tm, 64), lb=lb)
    conv_tile = lambda i, j: (0, jnp.clip(j - n_z, 0, n_x - 1))
    return pl.pallas_call(
        body,
        grid=(m // tm, n_main + 1),
        in_specs=[
            pl.BlockSpec((tm, d), lambda i, j: (i, 0), pipeline_mode=pl.Buffered(1)),
            pl.BlockSpec((1, N_ADA, d), lambda i, j: (i // per_batch, 0, 0)),
            pl.BlockSpec((1, d), lambda i, j: (0, 0)),
            pl.BlockSpec((None, d, tn), lambda i, j: (layer, 0, jnp.minimum(j, n_main - 1))),
            pl.BlockSpec((None, d, dtw), lambda i, j: (layer, 0, 0)),
            pl.BlockSpec((1, dtw), lambda i, j: (0, 0)),
            pl.BlockSpec((taps, tn), conv_tile),
            pl.BlockSpec((1, tn), conv_tile),
        ],
        out_specs=[
            pl.BlockSpec((tm, tn), lambda i, j: (i, jnp.minimum(j, n_zx - 1))),
            pl.BlockSpec((tm, tn), lambda i, j: (i, jnp.clip(j - n_zx, 0, n_gate - 1))),
            pl.BlockSpec((tm, tn // 2), lambda i, j: (i, jnp.clip(j - n_zx - n_gate, 0, n_glu - 1))),
            pl.BlockSpec((tm, dtw), lambda i, j: (i, 0)),
        ],
        out_shape=[
            jax.ShapeDtypeStruct((m, z_w + x_w), F32),
            jax.ShapeDtypeStruct((m, gate_w), F32),
            jax.ShapeDtypeStruct((m, glu_w), F32),
            jax.ShapeDtypeStruct((m, dtw), F32),
        ],
        scratch_shapes=[pltpu.VMEM((tm, d), BF16),
                        pltpu.VMEM((halo + tm, tn), F32),
                        pltpu.VMEM((len(_shift_residues(taps)), halo + tm, lb), F32),
                        pltpu.VMEM((n_x, halo, tn), F32)],
        compiler_params=_params(("arbitrary", "arbitrary")),
        name="inproj",
    )(x2, ada, nw.reshape(1, d), w_main, w_dt, dt_b, conv_w, conv_b.reshape(1, x_w))


def _conf_conv_body(u_ref, w_ref, b_ref, g_ref, beta_ref, o_ref, ext_ref, sh_ref, y_ref,
                    *, taps, rb, cb):
    rows, width = u_ref.shape[1], u_ref.shape[2]
    halo = _halo_rows(taps)
    first = pl.program_id(1) == 0

    @pl.when(first)
    def _():
        ext_ref[0:halo, :] = jnp.zeros((halo, width), F32)

    @pl.when(jnp.logical_not(first))
    def _():
        ext_ref[0:halo, :] = ext_ref[rows:rows + halo, :]

    ext_ref[halo:halo + rows, :] = u_ref[0]

    def lane_block(ci, carry):
        cols = pl.ds(pl.multiple_of(ci * cb, cb), cb)

        def emit(r0, acc):
            y_ref[r0:r0 + rb, cols] = acc

        _conv_lane_block(ext_ref, sh_ref, w_ref, b_ref, cols, taps, rows, rb, emit)
        return carry

    lax.fori_loop(0, width // cb, lane_block, 0)
    y = y_ref[...]
    mu = jnp.mean(y, axis=-1, keepdims=True)
    yc = y - mu
    var = jnp.mean(yc * yc, axis=-1, keepdims=True)
    yn = yc * lax.rsqrt(var + EPS) * g_ref[...] + beta_ref[...]
    o_ref[0] = _silu(yn).astype(BF16)


def _conf_conv(u3, dw_w, dw_b, ln_g, ln_b):
    bsz, seq, c = u3.shape
    taps = dw_w.shape[0]
    tr = _pick(seq, 256)
    cb = _pick(c, 256)
    halo = _halo_rows(taps)
    vec = lambda v: v.reshape(1, c)
    row_spec = pl.BlockSpec((1, c), lambda b, t: (0, 0))
    return pl.pallas_call(
        functools.partial(_conf_conv_body, taps=taps, rb=_pick(tr, 64), cb=cb),
        grid=(bsz, seq // tr),
        in_specs=[
            pl.BlockSpec((1, tr, c), lambda b, t: (b, t, 0)),
            pl.BlockSpec((taps, c), lambda b, t: (0, 0)),
            row_spec, row_spec, row_spec,
        ],
        out_specs=pl.BlockSpec((1, tr, c), lambda b, t: (b, t, 0)),
        out_shape=jax.ShapeDtypeStruct((bsz, seq, c), BF16),
        scratch_shapes=[pltpu.VMEM((tr + halo, c), F32),
                        pltpu.VMEM((len(_shift_residues(taps)), tr + halo, cb), F32),
                        pltpu.VMEM((tr, c), F32)],
        compiler_params=_params(("parallel", "arbitrary")),
        name="conf_conv",
    )(u3, dw_w, vec(dw_b), vec(ln_g), vec(ln_b))


def _split3(v):
    hi = v.astype(BF16)
    r1 = v - hi.astype(F32)
    mid = r1.astype(BF16)
    lo = (r1 - mid.astype(F32)).astype(BF16)
    return hi, mid, lo


def _pack_split(v, lo_half):
    hi = v.astype(BF16).astype(F32)
    r1 = v - hi
    mid = r1.astype(BF16).astype(F32)
    lo = r1 - mid
    first = jnp.where(lo_half, hi, pltpu.roll(mid, HEAD_DIM, 1))
    return jnp.concatenate([first.astype(BF16), lo.astype(BF16)], axis=1)


def _ssd_body(xs_ref, b_ref, c_ref, z_ref, dt_ref, alog_ref, dsk_ref, nw_ref, e_ref,
              yg_ref, ss_ref, s_ref, acsp_ref, dtp_ref, acst_ref, *, heads_per_group):
    t = pl.program_id(1)
    g = pl.program_id(2)
    q = SSD_CHUNK
    rows = xs_ref.shape[1]
    n_chunks = rows // q
    n_pairs = xs_ref.shape[2] // LANES
    causal = lax.broadcasted_iota(jnp.int32, (q, q), 0) >= lax.broadcasted_iota(jnp.int32, (q, q), 1)
    lane = lax.broadcasted_iota(jnp.int32, (q, LANES), 1)
    lo_half = lane < HEAD_DIM

    @pl.when(t == 0)
    def _():
        s_ref[g] = jnp.zeros(s_ref.shape[1:], F32)

    @pl.when(g == 0)
    def _():
        ss_ref[...] = jnp.zeros(ss_ref.shape, F32)
        tri = causal.astype(BF16)
        a2 = -jnp.exp(alog_ref[...]) * LOG2E
        for ci in range(n_chunks):
            rs = slice(ci * q, (ci + 1) * q)
            hi, mid, lo = _split3(dt_ref[0, rs, :] * a2)
            cs = _dot(tri, jnp.concatenate([hi, mid, lo], axis=1))
            acs = (cs[:, :LANES] + cs[:, LANES:2 * LANES]) + cs[:, 2 * LANES:]
            acsp_ref[ci] = _pack_split(acs, lo_half)
            dtp_ref[ci] = _pack_split(dt_ref[0, rs, :], lo_half)
            acst_ref[ci] = acs.T

    head_row0 = pl.multiple_of(g * heads_per_group, SUBLANES)
    n_bc = heads_per_group * LANES

    pre = []
    for ci in range(n_chunks):
        rs = slice(ci * q, (ci + 1) * q)
        spread = _dot(acsp_ref[ci], e_ref[...])
        dt_pairs = _dot(dtp_ref[ci], e_ref[:, n_bc:])
        acs_t = acst_ref[ci, pl.ds(head_row0, SUBLANES), :]
        bg = b_ref[0, rs, :].astype(BF16)
        cg = c_ref[0, rs, :].astype(BF16)
        cb = lax.dot_general(cg, bg, (((1,), (1,)), ((), ())), preferred_element_type=F32)
        y_diag, off_scale, xw_parts, decay_parts = [], [], [], []
        for p in range(n_pairs):
            ls = slice(p * LANES, (p + 1) * LANES)
            h0, h1 = 2 * p, 2 * p + 1
            bc0 = spread[:, h0 * LANES:(h0 + 1) * LANES]
            bc1 = spread[:, h1 * LANES:(h1 + 1) * LANES]
            m0 = cb * jnp.exp2(jnp.where(causal, bc0 - acs_t[h0:h0 + 1, :], -jnp.inf))
            m1 = cb * jnp.exp2(jnp.where(causal, bc1 - acs_t[h1:h1 + 1, :], -jnp.inf))
            acs_pair = spread[:, n_bc + p * LANES:n_bc + (p + 1) * LANES]
            xdt = xs_ref[0, rs, ls] * dt_pairs[:, ls]
            lhs = jnp.concatenate([m0.astype(BF16), m1.astype(BF16)], axis=1)
            rhs = jnp.concatenate([jnp.where(lo_half, xdt, 0.0).astype(BF16),
                                   jnp.where(lo_half, 0.0, xdt).astype(BF16)], axis=0)
            y_diag.append(_dot(lhs, rhs))
            off_scale.append(jnp.exp2(acs_pair))
            last_pair = acs_pair[q - 1:q, :]
            xw_parts.append((xdt * jnp.exp2(last_pair - acs_pair)).astype(BF16))
            decay_parts.append(jnp.exp2(last_pair))
        xw = jnp.concatenate(xw_parts, axis=1)
        new = lax.dot_general(bg, xw, (((0,), (0,)), ((), ())), preferred_element_type=F32)
        pre.append((cg, y_diag, off_scale, jnp.concatenate(decay_parts, axis=1), new))

    state = s_ref[g]
    for ci in range(n_chunks):
        rs = slice(ci * q, (ci + 1) * q)
        cg, y_diag, off_scale, decay, new = pre[ci]
        y_off = _dot(cg, state.astype(BF16))
        state = state * decay + new
        sq = jnp.zeros((q, LANES), F32)
        for p in range(n_pairs):
            ls = slice(p * LANES, (p + 1) * LANES)
            y = y_diag[p] + y_off[:, ls] * off_scale[p] + dsk_ref[0, :, ls] * xs_ref[0, rs, ls]
            v = y * _silu(z_ref[0, rs, ls])
            sq = sq + v * v
            yg_ref[0, rs, ls] = (v * nw_ref[:, ls]).astype(BF16)
        ss_ref[0, rs, :] += sq
    s_ref[g] = state


def _head_spread_matrices(groups, hpg):
    r = jnp.arange(2 * LANES)
    head_of_row = jnp.where(r < 3 * HEAD_DIM, r % HEAD_DIM, -1)
    n = jnp.arange(hpg * LANES + hpg * HEAD_DIM)
    local_head = jnp.where(n < hpg * LANES, n // LANES, (n - hpg * LANES) // HEAD_DIM)
    g = jnp.arange(groups)
    want = g[:, None, None] * hpg + local_head[None, None, :]
    return (head_of_row[None, :, None] == want).astype(BF16)


def _ssd(zx3, dt3, a_pad, dsk, nw, d_inner):
    bsz, seq, width = zx3.shape
    groups = SSM_GROUPS
    gw = d_inner // groups
    n_state = (width - 2 * d_inner) // (2 * groups)
    hpg = gw // HEAD_DIM
    assert gw % LANES == 0 and d_inner % n_state == 0 and hpg == SUBLANES
    assert dt3.shape[2] == LANES and groups * hpg <= HEAD_DIM
    spread = _head_spread_matrices(groups, hpg)
    tr = _pick(seq, 512)
    assert tr % SSD_CHUNK == 0
    x_off = d_inner // gw
    b_off = 2 * d_inner // n_state
    return pl.pallas_call(
        functools.partial(_ssd_body, heads_per_group=hpg),
        grid=(bsz, seq // tr, groups),
        in_specs=[
            pl.BlockSpec((1, tr, gw), lambda b, t, g: (b, t, x_off + g)),
            pl.BlockSpec((1, tr, n_state), lambda b, t, g: (b, t, b_off + g)),
            pl.BlockSpec((1, tr, n_state), lambda b, t, g: (b, t, b_off + groups + g)),
            pl.BlockSpec((1, tr, gw), lambda b, t, g: (b, t, g)),
            pl.BlockSpec((1, tr, LANES), lambda b, t, g: (b, t, 0)),
            pl.BlockSpec((1, LANES), lambda b, t, g: (0, 0)),
            pl.BlockSpec((1, 1, gw), lambda b, t, g: (g, 0, 0)),
            pl.BlockSpec((1, gw), lambda b, t, g: (0, g)),
            pl.BlockSpec((None,) + spread.shape[1:], lambda b, t, g: (g, 0, 0)),
        ],
        out_specs=[
            pl.BlockSpec((1, tr, gw), lambda b, t, g: (b, t, g)),
            pl.BlockSpec((1, tr, LANES), lambda b, t, g: (b, t, 0)),
        ],
        out_shape=[
            jax.ShapeDtypeStruct((bsz, seq, d_inner), BF16),
            jax.ShapeDtypeStruct((bsz, seq, LANES), F32),
        ],
        scratch_shapes=[pltpu.VMEM((groups, n_state, gw), F32),
                        pltpu.VMEM((tr // SSD_CHUNK, SSD_CHUNK, 2 * LANES), BF16),
                        pltpu.VMEM((tr // SSD_CHUNK, SSD_CHUNK, 2 * LANES), BF16),
                        pltpu.VMEM((tr // SSD_CHUNK, LANES, SSD_CHUNK), F32)],
        compiler_params=_params(("parallel", "arbitrary", "arbitrary")),
        name="ssd",
    )(zx3, zx3, zx3, zx3, dt3, a_pad, dsk, nw.reshape(1, d_inner), spread)


def _merge_body(yg_ref, ss_ref, uc_ref, gs_ref, gc_ref, x_ref, ada_ref, wso_ref, wpw_ref,
                bpw_ref, wo_ref, o_ref, *, ada_base, d_inner):
    j = pl.program_id(1)

    @pl.when(j == 0)
    def _():
        o_ref[...] = jnp.zeros(o_ref.shape, F32)

    mean_sq = jnp.sum(ss_ref[...], axis=-1, keepdims=True) * (1.0 / d_inner)
    y_ssm = _dot(yg_ref[...], wso_ref[...]) * lax.rsqrt(mean_sq + EPS)
    y_conv = _dot(uc_ref[...], wpw_ref[...]) + bpw_ref[...]
    merged = (gs_ref[...] * y_ssm + gc_ref[...] * y_conv).astype(BF16)
    o_ref[...] += _dot(merged, wo_ref[...])

    @pl.when(j == pl.num_programs(1) - 1)
    def _():
        o_ref[...] = x_ref[...] + ada_ref[0, ada_base + 2:ada_base + 3, :] * o_ref[...]


def _merge(yg2, ss2, uc2, gates2, x2, ada, ada_base, wso, wpw, bpw, wo, layer, seq):
    m, d = x2.shape
    d_inner = yg2.shape[1]
    c = uc2.shape[1]
    tm = _pick(seq, 512)
    tn = wso.shape[3]
    nn = d // tn
    per_batch = seq // tm
    return pl.pallas_call(
        functools.partial(_merge_body, ada_base=ada_base, d_inner=d_inner),
        grid=(m // tm, nn),
        in_specs=[
            pl.BlockSpec((tm, d_inner), lambda i, j: (i, 0)),
            pl.BlockSpec((tm, LANES), lambda i, j: (i, 0)),
            pl.BlockSpec((tm, c), lambda i, j: (i, 0)),
            pl.BlockSpec((tm, tn), lambda i, j: (i, j)),
            pl.BlockSpec((tm, tn), lambda i, j: (i, j + nn)),
            pl.BlockSpec((tm, d), lambda i, j: (i, 0)),
            pl.BlockSpec((1, N_ADA, d), lambda i, j: (i // per_batch, 0, 0)),
            pl.BlockSpec((None, None, d_inner, tn), lambda i, j: (layer, j, 0, 0)),
            pl.BlockSpec((None, None, c, tn), lambda i, j: (layer, j, 0, 0)),
            pl.BlockSpec((1, tn), lambda i, j: (0, j)),
            pl.BlockSpec((None, None, tn, d), lambda i, j: (layer, 0, j, 0)),
        ],
        out_specs=pl.BlockSpec((tm, d), lambda i, j: (i, 0)),
        out_shape=jax.ShapeDtypeStruct((m, d), F32),
        compiler_params=_params(("parallel", "arbitrary")),
        name="merge",
    )(yg2, ss2, uc2, gates2, gates2, x2, ada, wso, wpw, bpw.reshape(1, d), wo)


def _pad_lanes(v):
    return jnp.pad(v, [(0, 0)] * (v.ndim - 1) + [(0, LANES - v.shape[-1])])


def kernel(x, c, w_ada, b_ada, norm_ffn1, ffn1_w13, ffn1_w2, norm_mix, w_in, ssm_conv_w, ssm_conv_b, dt_bias, a_log, d_skip, ssm_norm_w, w_ssm_out, dw_w, dw_b, conv_ln_g, conv_ln_b, w_pw2, b_pw2, w_o, norm_ffn2, ffn2_w13, ffn2_w2, final_norm):
    bsz, seq, d = x.shape
    depth = w_ada.shape[0]
    groups = SSM_GROUPS
    d_inner = w_ssm_out.shape[1]
    conv_dim = ssm_conv_w.shape[2]
    heads = dt_bias.shape[1]
    conf_c = dw_w.shape[2]
    assert d_inner // heads == HEAD_DIM
    p_dt = d_inner + conv_dim
    p_glu = p_dt + heads
    p_gate = p_glu + 2 * conf_c
    gate_w = w_in.shape[2] - p_gate
    tn = _pick(math.gcd(d_inner, conv_dim, gate_w, 2 * conf_c), 1024)

    ada_all = _ada(c, w_ada, b_ada)
    tf = _pick(ffn1_w2.shape[1], 512)
    w13_1, w2_1 = _to_bf16(ffn1_w13, tf), _to_bf16(ffn1_w2, d)
    w13_2, w2_2 = _to_bf16(ffn2_w13, tf), _to_bf16(ffn2_w2, d)
    tmn = _pick(d, 512)
    wso, wpw, wo = _to_bf16(w_ssm_out, tmn), _to_bf16(w_pw2, tmn), _to_bf16(w_o, d)
    w_main, w_dt = _repack_w_in(w_in, p_dt, heads, conf_c, gate_w, tn // 2)
    x2 = x.reshape(bsz * seq, d)
    for l in range(depth):
        ada = ada_all[l]
        dt_b = _pad_lanes(dt_bias[l]).reshape(1, LANES)
        a_pad = _pad_lanes(a_log[l]).reshape(1, LANES)
        dsk = jnp.repeat(d_skip[l], HEAD_DIM).reshape(groups, 1, d_inner // groups)

        x2 = _ffn(x2, ada, 0, norm_ffn1[l], w13_1, w2_1, l, final_norm, seq, False)

        zx, gates, u, dt = _inproj(x2, ada, 3, norm_mix[l], w_main, w_dt, l, dt_b,
                                   ssm_conv_w[l], ssm_conv_b[l], (d_inner, gate_w, conf_c), seq, tn)

        yg, ss = _ssd(zx.reshape(bsz, seq, p_dt), dt.reshape(bsz, seq, LANES), a_pad, dsk,
                      ssm_norm_w[l], d_inner)
        uc = _conf_conv(u.reshape(bsz, seq, conf_c), dw_w[l], dw_b[l], conv_ln_g[l], conv_ln_b[l])

        x2 = _merge(yg.reshape(bsz * seq, d_inner), ss.reshape(bsz * seq, LANES),
                    uc.reshape(bsz * seq, conf_c), gates, x2, ada, 3,
                    wso, wpw, b_pw2[l], wo, l, seq)

        x2 = _ffn(x2, ada, 6, norm_ffn2[l], w13_2, w2_2, l, final_norm, seq, l == depth - 1)
    return x2.reshape(bsz, seq, d)
```

```python
import functools
import math

import jax
import jax.numpy as jnp
from jax import lax
from jax.experimental import pallas as pl
from jax.experimental.pallas import tpu as pltpu

F32 = jnp.float32
BF16 = jnp.bfloat16
EPS = 1e-6
LOG2E = 1.4426950408889634

SSM_GROUPS = 8
N_ADA = 9

LANES = 128
SUBLANES = 8
VMEM_LIMIT_BYTES = 60 * 1024 * 1024

SSD_CHUNK = 128
HEAD_DIM = 64
CAST_BLOCK_BYTES = 8 * 1024 * 1024


def _params(semantics):
    return pltpu.CompilerParams(dimension_semantics=semantics,
                                vmem_limit_bytes=VMEM_LIMIT_BYTES)


def _pick(total, want):
    t = min(total, want)
    while total % t:
        t //= 2
    return t


def _silu(v):
    return v * jax.nn.sigmoid(v)


def _softplus(v):
    return jnp.maximum(v, 0.0) + jnp.log1p(jnp.exp(-jnp.abs(v)))


def _rms_mod(x, nw, shift, scale):
    y = x * lax.rsqrt(jnp.mean(x * x, axis=-1, keepdims=True) + EPS) * nw
    return y * (1.0 + scale) + shift


def _dot(a, b):
    return jnp.dot(a, b, preferred_element_type=F32)


def _cast_body(w_ref, o_ref):
    tc = o_ref.shape[-1]
    for t in range(o_ref.shape[0]):
        o_ref[t] = w_ref[:, t * tc:(t + 1) * tc].astype(BF16)


def _to_bf16(w, tc):
    depth, rows, cols = w.shape
    want = 1 << int(math.log2(max(SUBLANES, CAST_BLOCK_BYTES // (4 * cols))))
    tr = _pick(rows, want)
    nt = cols // tc
    per_layer = rows // tr
    return pl.pallas_call(
        _cast_body,
        grid=(depth * per_layer,),
        in_specs=[pl.BlockSpec((None, tr, cols), lambda i: (i // per_layer, i % per_layer, 0))],
        out_specs=pl.BlockSpec((None, nt, tr, tc), lambda i: (i // per_layer, 0, i % per_layer, 0)),
        out_shape=jax.ShapeDtypeStruct((depth, nt, rows, tc), BF16),
        compiler_params=_params(("parallel",)),
        name="cast",
    )(w)


def _repack_body(w_ref, o_ref, odt_ref, *, p_dt, heads, conf_c, gate_w, half):
    p_glu = p_dt + heads
    p_gate = p_glu + 2 * conf_c
    o_ref[:, 0:p_dt] = w_ref[:, 0:p_dt].astype(BF16)
    o_ref[:, p_dt:p_dt + gate_w] = w_ref[:, p_gate:p_gate + gate_w].astype(BF16)
    for i in range(conf_c // half):
        dst = p_dt + gate_w + 2 * i * half
        o_ref[:, dst:dst + half] = w_ref[:, p_glu + i * half:p_glu + (i + 1) * half].astype(BF16)
        o_ref[:, dst + half:dst + 2 * half] = (
            w_ref[:, p_glu + conf_c + i * half:p_glu + conf_c + (i + 1) * half].astype(BF16))
    odt_ref[:, 0:heads] = w_ref[:, p_dt:p_glu].astype(BF16)
    odt_ref[:, heads:] = jnp.zeros((odt_ref.shape[0], odt_ref.shape[1] - heads), BF16)


def _repack_w_in(w_in, p_dt, heads, conf_c, gate_w, half):
    depth, d, cols = w_in.shape
    tr = _pick(d, 128)
    per_layer = d // tr
    n_main = cols - heads
    return pl.pallas_call(
        functools.partial(_repack_body, p_dt=p_dt, heads=heads, conf_c=conf_c, gate_w=gate_w, half=half),
        grid=(depth * per_layer,),
        in_specs=[pl.BlockSpec((None, tr, cols), lambda i: (i // per_layer, i % per_layer, 0))],
        out_specs=[pl.BlockSpec((None, tr, n_main), lambda i: (i // per_layer, i % per_layer, 0)),
                   pl.BlockSpec((None, tr, LANES), lambda i: (i // per_layer, i % per_layer, 0))],
        out_shape=[jax.ShapeDtypeStruct((depth, d, n_main), BF16),
                   jax.ShapeDtypeStruct((depth, d, LANES), BF16)],
        compiler_params=_params(("parallel",)),
        name="repack",
    )(w_in)


def _ada_body(c_ref, w_ref, b_ref, o_ref):
    c = c_ref[...]
    o_ref[0] = _dot(_silu(c).astype(BF16), w_ref[0].astype(BF16)) + b_ref[0]


def _ada(c, w_ada, b_ada):
    depth, d, n = w_ada.shape
    bsz = c.shape[0]
    rows = -(-bsz // SUBLANES) * SUBLANES
    c_pad = jnp.pad(c, ((0, rows - bsz), (0, 0)))
    tn = _pick(n, 1024)
    out = pl.pallas_call(
        _ada_body,
        grid=(depth, n // tn),
        in_specs=[
            pl.BlockSpec((rows, d), lambda l, j: (0, 0)),
            pl.BlockSpec((1, d, tn), lambda l, j: (l, 0, j)),
            pl.BlockSpec((1, 1, tn), lambda l, j: (l, 0, j)),
        ],
        out_specs=pl.BlockSpec((1, rows, tn), lambda l, j: (l, 0, j)),
        out_shape=jax.ShapeDtypeStruct((depth, rows, n), F32),
        compiler_params=_params(("parallel", "parallel")),
        name="ada",
    )(c_pad, w_ada, b_ada.reshape(depth, 1, n))
    return out[:, :bsz].reshape(depth, bsz, N_ADA, d)


def _ffn_body(x_ref, ada_ref, nw_ref, w1_ref, w3_ref, w2_ref, fw_ref, o_ref, h_ref,
              *, ada_base, final):
    j = pl.program_id(1)

    @pl.when(j == 0)
    def _():
        shift = ada_ref[0, ada_base:ada_base + 1, :]
        scale = ada_ref[0, ada_base + 1:ada_base + 2, :]
        h_ref[...] = _rms_mod(x_ref[...], nw_ref[...], shift, scale).astype(BF16)
        o_ref[...] = jnp.zeros(o_ref.shape, F32)

    h = h_ref[...]
    a = _dot(h, w1_ref[...])
    g = _dot(h, w3_ref[...])
    u = (_silu(g) * a).astype(BF16)
    o_ref[...] += _dot(u, w2_ref[...])

    @pl.when(j == pl.num_programs(1) - 1)
    def _():
        gate = ada_ref[0, ada_base + 2:ada_base + 3, :]
        y = x_ref[...] + 0.5 * gate * o_ref[...]
        if final:
            y = y * lax.rsqrt(jnp.mean(y * y, axis=-1, keepdims=True) + EPS) * fw_ref[...]
        o_ref[...] = y


def _ffn(x2, ada, ada_base, nw, w13, w2, layer, fw, seq, final):
    m, d = x2.shape
    tf = w13.shape[3]
    nf = w13.shape[1] // 2
    assert w2.shape[2] == nf * tf and w2.shape[3] == d
    tm = _pick(seq, 1024)
    per_batch = seq // tm
    return pl.pallas_call(
        functools.partial(_ffn_body, ada_base=ada_base, final=final),
        grid=(m // tm, nf),
        in_specs=[
            pl.BlockSpec((tm, d), lambda i, j: (i, 0), pipeline_mode=pl.Buffered(1)),
            pl.BlockSpec((1, N_ADA, d), lambda i, j: (i // per_batch, 0, 0)),
            pl.BlockSpec((1, d), lambda i, j: (0, 0)),
            pl.BlockSpec((None, None, d, tf), lambda i, j: (layer, j, 0, 0)),
            pl.BlockSpec((None, None, d, tf), lambda i, j: (layer, j + nf, 0, 0)),
            pl.BlockSpec((None, None, tf, d), lambda i, j: (layer, 0, j, 0)),
            pl.BlockSpec((1, d), lambda i, j: (0, 0)),
        ],
        out_specs=pl.BlockSpec((tm, d), lambda i, j: (i, 0)),
        out_shape=jax.ShapeDtypeStruct((m, d), F32),
        scratch_shapes=[pltpu.VMEM((tm, d), BF16)],
        compiler_params=_params(("parallel", "arbitrary")),
        name="ffn",
    )(x2, ada, nw.reshape(1, d), w13, w13, w2, fw.reshape(1, d))


def _halo_rows(taps):
    return -(-(taps - 1) // SUBLANES) * SUBLANES


def _shift_residues(taps):
    base = _halo_rows(taps) - (taps - 1)
    return sorted({(base + k) % SUBLANES for k in range(taps)} - {0})


def _conv_lane_block(ext_ref, sh_ref, w_ref, b_ref, cols, taps, rows, rb, emit):
    halo = _halo_rows(taps)
    base = halo - (taps - 1)
    span = halo + rows - SUBLANES
    slot = {s: i for i, s in enumerate(_shift_residues(taps))}
    for s, i in slot.items():
        sh_ref[i, 0:span, :] = ext_ref[s:s + span, cols]
    for r0 in range(0, rows, rb):
        acc = None
        for k in range(taps):
            s = (base + k) % SUBLANES
            a = base + k - s + r0
            win = ext_ref[a:a + rb, cols] if s == 0 else sh_ref[slot[s], a:a + rb, :]
            term = w_ref[k:k + 1, cols] * win
            acc = term if acc is None else acc + term
        emit(r0, acc + b_ref[:, cols])


def _inproj_body(x_ref, ada_ref, nw_ref, w_ref, wdt_ref, dtb_ref, cw_ref, cb_ref,
                 zx_ref, gate_ref, u_ref, dt_ref, h_ref, ext_ref, sh_ref, carry_ref,
                 *, ada_base, n_z, n_x, n_gate, n_glu, per_batch, taps, rb, lb):
    i = pl.program_id(0)
    j = pl.program_id(1)
    n_zx = n_z + n_x
    n_main = n_zx + n_gate + n_glu
    tm, tn = zx_ref.shape
    halo = _halo_rows(taps)

    @pl.when(j == 0)
    def _():
        shift = ada_ref[0, ada_base:ada_base + 1, :]
        scale = ada_ref[0, ada_base + 1:ada_base + 2, :]
        h_ref[...] = _rms_mod(x_ref[...], nw_ref[...], shift, scale).astype(BF16)

    @pl.when(j < n_z)
    def _():
        zx_ref[...] = _dot(h_ref[...], w_ref[...])

    @pl.when(jnp.logical_and(j >= n_z, j < n_zx))
    def _():
        xt = j - n_z

        @pl.when(i % per_batch == 0)
        def _():
            ext_ref[0:halo, :] = jnp.zeros((halo, tn), F32)

        @pl.when(i % per_batch != 0)
        def _():
            ext_ref[0:halo, :] = carry_ref[xt]

        ext_ref[halo:halo + tm, :] = _dot(h_ref[...], w_ref[...])
        carry_ref[xt] = ext_ref[tm:tm + halo, :]
        for c0 in range(0, tn, lb):
            cols = slice(c0, c0 + lb)

            def emit(r0, acc, cols=cols):
                zx_ref[r0:r0 + rb, cols] = _silu(acc)

            _conv_lane_block(ext_ref, sh_ref, cw_ref, cb_ref, cols, taps, tm, rb, emit)

    @pl.when(jnp.logical_and(j >= n_zx, j < n_zx + n_gate))
    def _():
        gate_ref[...] = jax.nn.sigmoid(_dot(h_ref[...], w_ref[...]))

    @pl.when(jnp.logical_and(j >= n_zx + n_gate, j < n_main))
    def _():
        acc = _dot(h_ref[...], w_ref[...])
        half = acc.shape[1] // 2
        u_ref[...] = acc[:, :half] * jax.nn.sigmoid(acc[:, half:])

    @pl.when(j == n_main)
    def _():
        dt_ref[...] = _softplus(_dot(h_ref[...], wdt_ref[...]) + dtb_ref[...])


def _inproj(x2, ada, ada_base, nw, w_main, w_dt, layer, dt_b, conv_w, conv_b, widths, seq, tn):
    m, d = x2.shape
    z_w, gate_w, glu_w = widths
    taps, x_w = conv_w.shape
    n_z, n_x, n_gate, n_glu = z_w // tn, x_w // tn, gate_w // tn, 2 * glu_w // tn
    n_zx = n_z + n_x
    n_main = n_zx + n_gate + n_glu
    tm = _pick(seq, 1024)
    per_batch = seq // tm
    dtw = w_dt.shape[2]
    halo = _halo_rows(taps)
    lb = _pick(tn, 256)
    body = functools.partial(_inproj_body, ada_base=ada_base, n_z=n_z, n_x=n_x, n_gate=n_gate,
                             n_glu=n_glu, per_batch=per_batch, taps=taps, rb=_pick(tm, 64), lb=lb)
    conv_tile = lambda i, j: (0, jnp.clip(j - n_z, 0, n_x - 1))
    return pl.pallas_call(
        body,
        grid=(m // tm, n_main + 1),
        in_specs=[
            pl.BlockSpec((tm, d), lambda i, j: (i, 0), pipeline_mode=pl.Buffered(1)),
            pl.BlockSpec((1, N_ADA, d), lambda i, j: (i // per_batch, 0, 0)),
            pl.BlockSpec((1, d), lambda i, j: (0, 0)),
            pl.BlockSpec((None, d, tn), lambda i, j: (layer, 0, jnp.minimum(j, n_main - 1))),
            pl.BlockSpec((None, d, dtw), lambda i, j: (layer, 0, 0)),
            pl.BlockSpec((1, dtw), lambda i, j: (0, 0)),
            pl.BlockSpec((taps, tn), conv_tile),
            pl.BlockSpec((1, tn), conv_tile),
        ],
        out_specs=[
            pl.BlockSpec((tm, tn), lambda i, j: (i, jnp.minimum(j, n_zx - 1))),
            pl.BlockSpec((tm, tn), lambda i, j: (i, jnp.clip(j - n_zx, 0, n_gate - 1))),
            pl.BlockSpec((tm, tn // 2), lambda i, j: (i, jnp.clip(j - n_zx - n_gate, 0, n_glu - 1))),
            pl.BlockSpec((tm, dtw), lambda i, j: (i, 0)),
        ],
        out_shape=[
            jax.ShapeDtypeStruct((m, z_w + x_w), F32),
            jax.ShapeDtypeStruct((m, gate_w), F32),
            jax.ShapeDtypeStruct((m, glu_w), F32),
            jax.ShapeDtypeStruct((m, dtw), F32),
        ],
        scratch_shapes=[pltpu.VMEM((tm, d), BF16),
                        pltpu.VMEM((halo + tm, tn), F32),
                        pltpu.VMEM((len(_shift_residues(taps)), halo + tm, lb), F32),
                        pltpu.VMEM((n_x, halo, tn), F32)],
        compiler_params=_params(("arbitrary", "arbitrary")),
        name="inproj",
    )(x2, ada, nw.reshape(1, d), w_main, w_dt, dt_b, conv_w, conv_b.reshape(1, x_w))


def _conf_conv_body(u_ref, w_ref, b_ref, g_ref, beta_ref, o_ref, ext_ref, sh_ref, y_ref,
                    *, taps, rb, cb):
    rows, width = u_ref.shape[1], u_ref.shape[2]
    halo = _halo_rows(taps)
    first = pl.program_id(1) == 0

    @pl.when(first)
    def _():
        ext_ref[0:halo, :] = jnp.zeros((halo, width), F32)

    @pl.when(jnp.logical_not(first))
    def _():
        ext_ref[0:halo, :] = ext_ref[rows:rows + halo, :]

    ext_ref[halo:halo + rows, :] = u_ref[0]

    def lane_block(ci, carry):
        cols = pl.ds(pl.multiple_of(ci * cb, cb), cb)

        def emit(r0, acc):
            y_ref[r0:r0 + rb, cols] = acc

        _conv_lane_block(ext_ref, sh_ref, w_ref, b_ref, cols, taps, rows, rb, emit)
        return carry

    lax.fori_loop(0, width // cb, lane_block, 0)
    y = y_ref[...]
    mu = jnp.mean(y, axis=-1, keepdims=True)
    yc = y - mu
    var = jnp.mean(yc * yc, axis=-1, keepdims=True)
    yn = yc * lax.rsqrt(var + EPS) * g_ref[...] + beta_ref[...]
    o_ref[0] = _silu(yn).astype(BF16)


def _conf_conv(u3, dw_w, dw_b, ln_g, ln_b):
    bsz, seq, c = u3.shape
    taps = dw_w.shape[0]
    tr = _pick(seq, 256)
    cb = _pick(c, LANES)
    halo = _halo_rows(taps)
    vec = lambda v: v.reshape(1, c)
    row_spec = pl.BlockSpec((1, c), lambda b, t: (0, 0))
    return pl.pallas_call(
        functools.partial(_conf_conv_body, taps=taps, rb=_pick(tr, 128), cb=cb),
        grid=(bsz, seq // tr),
        in_specs=[
            pl.BlockSpec((1, tr, c), lambda b, t: (b, t, 0)),
            pl.BlockSpec((taps, c), lambda b, t: (0, 0)),
            row_spec, row_spec, row_spec,
        ],
        out_specs=pl.BlockSpec((1, tr, c), lambda b, t: (b, t, 0)),
        out_shape=jax.ShapeDtypeStruct((bsz, seq, c), BF16),
        scratch_shapes=[pltpu.VMEM((tr + halo, c), F32),
                        pltpu.VMEM((len(_shift_residues(taps)), tr + halo, cb), F32),
                        pltpu.VMEM((tr, c), F32)],
        compiler_params=_params(("parallel", "arbitrary")),
        name="conf_conv",
    )(u3, dw_w, vec(dw_b), vec(ln_g), vec(ln_b))


def _split3(v):
    hi = v.astype(BF16)
    r1 = v - hi.astype(F32)
    mid = r1.astype(BF16)
    lo = (r1 - mid.astype(F32)).astype(BF16)
    return hi, mid, lo


def _pack_split(v, lo_half):
    hi = v.astype(BF16).astype(F32)
    r1 = v - hi
    mid = r1.astype(BF16).astype(F32)
    lo = r1 - mid
    first = jnp.where(lo_half, hi, pltpu.roll(mid, HEAD_DIM, 1))
    return jnp.concatenate([first.astype(BF16), lo.astype(BF16)], axis=1)


def _ssd_body(xs_ref, b_ref, c_ref, z_ref, dt_ref, alog_ref, dsk_ref, nw_ref, e_ref,
              yg_ref, ss_ref, s_ref, acsp_ref, dtp_ref, acst_ref, *, heads_per_group):
    t = pl.program_id(1)
    g = pl.program_id(2)
    q = SSD_CHUNK
    rows = xs_ref.shape[1]
    n_chunks = rows // q
    n_pairs = xs_ref.shape[2] // LANES
    causal = lax.broadcasted_iota(jnp.int32, (q, q), 0) >= lax.broadcasted_iota(jnp.int32, (q, q), 1)
    lane = lax.broadcasted_iota(jnp.int32, (q, LANES), 1)
    lo_half = lane < HEAD_DIM

    @pl.when(t == 0)
    def _():
        s_ref[g] = jnp.zeros(s_ref.shape[1:], F32)

    @pl.when(g == 0)
    def _():
        ss_ref[...] = jnp.zeros(ss_ref.shape, F32)
        tri = causal.astype(BF16)
        a2 = -jnp.exp(alog_ref[...]) * LOG2E
        for ci in range(n_chunks):
            rs = slice(ci * q, (ci + 1) * q)
            hi, mid, lo = _split3(dt_ref[0, rs, :] * a2)
            cs = _dot(tri, jnp.concatenate([hi, mid, lo], axis=1))
            acs = (cs[:, :LANES] + cs[:, LANES:2 * LANES]) + cs[:, 2 * LANES:]
            acsp_ref[ci] = _pack_split(acs, lo_half)
            dtp_ref[ci] = _pack_split(dt_ref[0, rs, :], lo_half)
            acst_ref[ci] = acs.T

    head_row0 = pl.multiple_of(g * heads_per_group, SUBLANES)
    n_bc = heads_per_group * LANES

    pre = []
    for ci in range(n_chunks):
        rs = slice(ci * q, (ci + 1) * q)
        spread = _dot(acsp_ref[ci], e_ref[...])
        dt_pairs = _dot(dtp_ref[ci], e_ref[:, n_bc:])
        acs_t = acst_ref[ci, pl.ds(head_row0, SUBLANES), :]
        bg = b_ref[0, rs, :].astype(BF16)
        cg = c_ref[0, rs, :].astype(BF16)
        cb = lax.dot_general(cg, bg, (((1,), (1,)), ((), ())), preferred_element_type=F32)
        y_diag, off_scale, xw_parts, decay_parts = [], [], [], []
        for p in range(n_pairs):
            ls = slice(p * LANES, (p + 1) * LANES)
            h0, h1 = 2 * p, 2 * p + 1
            bc0 = spread[:, h0 * LANES:(h0 + 1) * LANES]
            bc1 = spread[:, h1 * LANES:(h1 + 1) * LANES]
            m0 = cb * jnp.exp2(jnp.where(causal, bc0 - acs_t[h0:h0 + 1, :], -jnp.inf))
            m1 = cb * jnp.exp2(jnp.where(causal, bc1 - acs_t[h1:h1 + 1, :], -jnp.inf))
            acs_pair = spread[:, n_bc + p * LANES:n_bc + (p + 1) * LANES]
            xdt = xs_ref[0, rs, ls] * dt_pairs[:, ls]
            lhs = jnp.concatenate([m0.astype(BF16), m1.astype(BF16)], axis=1)
            rhs = jnp.concatenate([jnp.where(lo_half, xdt, 0.0).astype(BF16),
                                   jnp.where(lo_half, 0.0, xdt).astype(BF16)], axis=0)
            y_diag.append(_dot(lhs, rhs))
            off_scale.append(jnp.exp2(acs_pair))
            last_pair = acs_pair[q - 1:q, :]
            xw_parts.append((xdt * jnp.exp2(last_pair - acs_pair)).astype(BF16))
            decay_parts.append(jnp.exp2(last_pair))
        xw = jnp.concatenate(xw_parts, axis=1)
        new = lax.dot_general(bg, xw, (((0,), (0,)), ((), ())), preferred_element_type=F32)
        pre.append((cg, y_diag, off_scale, jnp.concatenate(decay_parts, axis=1), new))

    state = s_ref[g]
    for ci in range(n_chunks):
        rs = slice(ci * q, (ci + 1) * q)
        cg, y_diag, off_scale, decay, new = pre[ci]
        y_off = _dot(cg, state.astype(BF16))
        state = state * decay + new
        sq = jnp.zeros((q, LANES), F32)
        for p in range(n_pairs):
            ls = slice(p * LANES, (p + 1) * LANES)
            y = y_diag[p] + y_off[:, ls] * off_scale[p] + dsk_ref[0, :, ls] * xs_ref[0, rs, ls]
            v = y * _silu(z_ref[0, rs, ls])
            sq = sq + v * v
            yg_ref[0, rs, ls] = (v * nw_ref[:, ls]).astype(BF16)
        ss_ref[0, rs, :] += sq
    s_ref[g] = state


def _head_spread_matrices(groups, hpg):
    r = jnp.arange(2 * LANES)
    head_of_row = jnp.where(r < 3 * HEAD_DIM, r % HEAD_DIM, -1)
    n = jnp.arange(hpg * LANES + hpg * HEAD_DIM)
    local_head = jnp.where(n < hpg * LANES, n // LANES, (n - hpg * LANES) // HEAD_DIM)
    g = jnp.arange(groups)
    want = g[:, None, None] * hpg + local_head[None, None, :]
    return (head_of_row[None, :, None] == want).astype(BF16)


def _ssd(zx3, dt3, a_pad, dsk, nw, d_inner):
    bsz, seq, width = zx3.shape
    groups = SSM_GROUPS
    gw = d_inner // groups
    n_state = (width - 2 * d_inner) // (2 * groups)
    hpg = gw // HEAD_DIM
    assert gw % LANES == 0 and d_inner % n_state == 0 and hpg == SUBLANES
    assert dt3.shape[2] == LANES and groups * hpg <= HEAD_DIM
    spread = _head_spread_matrices(groups, hpg)
    tr = _pick(seq, 512)
    assert tr % SSD_CHUNK == 0
    x_off = d_inner // gw
    b_off = 2 * d_inner // n_state
    return pl.pallas_call(
        functools.partial(_ssd_body, heads_per_group=hpg),
        grid=(bsz, seq // tr, groups),
        in_specs=[
            pl.BlockSpec((1, tr, gw), lambda b, t, g: (b, t, x_off + g)),
            pl.BlockSpec((1, tr, n_state), lambda b, t, g: (b, t, b_off + g)),
            pl.BlockSpec((1, tr, n_state), lambda b, t, g: (b, t, b_off + groups + g)),
            pl.BlockSpec((1, tr, gw), lambda b, t, g: (b, t, g)),
            pl.BlockSpec((1, tr, LANES), lambda b, t, g: (b, t, 0)),
            pl.BlockSpec((1, LANES), lambda b, t, g: (0, 0)),
            pl.BlockSpec((1, 1, gw), lambda b, t, g: (g, 0, 0)),
            pl.BlockSpec((1, gw), lambda b, t, g: (0, g)),
            pl.BlockSpec((None,) + spread.shape[1:], lambda b, t, g: (g, 0, 0)),
        ],
        out_specs=[
            pl.BlockSpec((1, tr, gw), lambda b, t, g: (b, t, g)),
            pl.BlockSpec((1, tr, LANES), lambda b, t, g: (b, t, 0)),
        ],
        out_shape=[
            jax.ShapeDtypeStruct((bsz, seq, d_inner), BF16),
            jax.ShapeDtypeStruct((bsz, seq, LANES), F32),
        ],
        scratch_shapes=[pltpu.VMEM((groups, n_state, gw), F32),
                        pltpu.VMEM((tr // SSD_CHUNK, SSD_CHUNK, 2 * LANES), BF16),
                        pltpu.VMEM((tr // SSD_CHUNK, SSD_CHUNK, 2 * LANES), BF16),
                        pltpu.VMEM((tr // SSD_CHUNK, LANES, SSD_CHUNK), F32)],
        compiler_params=_params(("parallel", "arbitrary", "arbitrary")),
        name="ssd",
    )(zx3, zx3, zx3, zx3, dt3, a_pad, dsk, nw.reshape(1, d_inner), spread)


def _merge_body(yg_ref, ss_ref, uc_ref, gs_ref, gc_ref, x_ref, ada_ref, wso_ref, wpw_ref,
                bpw_ref, wo_ref, o_ref, *, ada_base, d_inner):
    j = pl.program_id(1)

    @pl.when(j == 0)
    def _():
        o_ref[...] = jnp.zeros(o_ref.shape, F32)

    mean_sq = jnp.sum(ss_ref[...], axis=-1, keepdims=True) * (1.0 / d_inner)
    y_ssm = _dot(yg_ref[...], wso_ref[...]) * lax.rsqrt(mean_sq + EPS)
    y_conv = _dot(uc_ref[...], wpw_ref[...]) + bpw_ref[...]
    merged = (gs_ref[...] * y_ssm + gc_ref[...] * y_conv).astype(BF16)
    o_ref[...] += _dot(merged, wo_ref[...])

    @pl.when(j == pl.num_programs(1) - 1)
    def _():
        o_ref[...] = x_ref[...] + ada_ref[0, ada_base + 2:ada_base + 3, :] * o_ref[...]


def _merge(yg2, ss2, uc2, gates2, x2, ada, ada_base, wso, wpw, bpw, wo, layer, seq):
    m, d = x2.shape
    d_inner = yg2.shape[1]
    c = uc2.shape[1]
    tm = _pick(seq, 512)
    tn = wso.shape[3]
    nn = d // tn
    per_batch = seq // tm
    return pl.pallas_call(
        functools.partial(_merge_body, ada_base=ada_base, d_inner=d_inner),
        grid=(m // tm, nn),
        in_specs=[
            pl.BlockSpec((tm, d_inner), lambda i, j: (i, 0)),
            pl.BlockSpec((tm, LANES), lambda i, j: (i, 0)),
            pl.BlockSpec((tm, c), lambda i, j: (i, 0)),
            pl.BlockSpec((tm, tn), lambda i, j: (i, j)),
            pl.BlockSpec((tm, tn), lambda i, j: (i, j + nn)),
            pl.BlockSpec((tm, d), lambda i, j: (i, 0)),
            pl.BlockSpec((1, N_ADA, d), lambda i, j: (i // per_batch, 0, 0)),
            pl.BlockSpec((None, None, d_inner, tn), lambda i, j: (layer, j, 0, 0)),
            pl.BlockSpec((None, None, c, tn), lambda i, j: (layer, j, 0, 0)),
            pl.BlockSpec((1, tn), lambda i, j: (0, j)),
            pl.BlockSpec((None, None, tn, d), lambda i, j: (layer, 0, j, 0)),
        ],
        out_specs=pl.BlockSpec((tm, d), lambda i, j: (i, 0)),
        out_shape=jax.ShapeDtypeStruct((m, d), F32),
        compiler_params=_params(("parallel", "arbitrary")),
        name="merge",
    )(yg2, ss2, uc2, gates2, gates2, x2, ada, wso, wpw, bpw.reshape(1, d), wo)


def _pad_lanes(v):
    return jnp.pad(v, [(0, 0)] * (v.ndim - 1) + [(0, LANES - v.shape[-1])])


def kernel(x, c, w_ada, b_ada, norm_ffn1, ffn1_w13, ffn1_w2, norm_mix, w_in, ssm_conv_w, ssm_conv_b, dt_bias, a_log, d_skip, ssm_norm_w, w_ssm_out, dw_w, dw_b, conv_ln_g, conv_ln_b, w_pw2, b_pw2, w_o, norm_ffn2, ffn2_w13, ffn2_w2, final_norm):
    bsz, seq, d = x.shape
    depth = w_ada.shape[0]
    groups = SSM_GROUPS
    d_inner = w_ssm_out.shape[1]
    conv_dim = ssm_conv_w.shape[2]
    heads = dt_bias.shape[1]
    conf_c = dw_w.shape[2]
    assert d_inner // heads == HEAD_DIM
    p_dt = d_inner + conv_dim
    p_glu = p_dt + heads
    p_gate = p_glu + 2 * conf_c
    gate_w = w_in.shape[2] - p_gate
    tn = _pick(math.gcd(d_inner, conv_dim, gate_w, 2 * conf_c), 1024)

    ada_all = _ada(c, w_ada, b_ada)
    tf = _pick(ffn1_w2.shape[1], 512)
    w13_1, w2_1 = _to_bf16(ffn1_w13, tf), _to_bf16(ffn1_w2, d)
    w13_2, w2_2 = _to_bf16(ffn2_w13, tf), _to_bf16(ffn2_w2, d)
    tmn = _pick(d, 512)
    wso, wpw, wo = _to_bf16(w_ssm_out, tmn), _to_bf16(w_pw2, tmn), _to_bf16(w_o, d)
    w_main, w_dt = _repack_w_in(w_in, p_dt, heads, conf_c, gate_w, tn // 2)
    x2 = x.reshape(bsz * seq, d)
    for l in range(depth):
        ada = ada_all[l]
        dt_b = _pad_lanes(dt_bias[l]).reshape(1, LANES)
        a_pad = _pad_lanes(a_log[l]).reshape(1, LANES)
        dsk = jnp.repeat(d_skip[l], HEAD_DIM).reshape(groups, 1, d_inner // groups)

        x2 = _ffn(x2, ada, 0, norm_ffn1[l], w13_1, w2_1, l, final_norm, seq, False)

        zx, gates, u, dt = _inproj(x2, ada, 3, norm_mix[l], w_main, w_dt, l, dt_b,
                                   ssm_conv_w[l], ssm_conv_b[l], (d_inner, gate_w, conf_c), seq, tn)

        yg, ss = _ssd(zx.reshape(bsz, seq, p_dt), dt.reshape(bsz, seq, LANES), a_pad, dsk,
                      ssm_norm_w[l], d_inner)
        uc = _conf_conv(u.reshape(bsz, seq, conf_c), dw_w[l], dw_b[l], conv_ln_g[l], conv_ln_b[l])

        x2 = _merge(yg.reshape(bsz * seq, d_inner), ss.reshape(bsz * seq, LANES),
                    uc.reshape(bsz * seq, conf_c), gates, x2, ada, 3,
                    wso, wpw, b_pw2[l], wo, l, seq)

        x2 = _ffn(x2, ada, 6, norm_ffn2[l], w13_2, w2_2, l, final_norm, seq, l == depth - 1)
    return x2.reshape(bsz, seq, d)
```

```python
import functools
import math

import jax
import jax.numpy as jnp
from jax import lax
from jax.experimental import pallas as pl
from jax.experimental.pallas import tpu as pltpu

F32 = jnp.float32
BF16 = jnp.bfloat16
EPS = 1e-6
LOG2E = 1.4426950408889634

SSM_GROUPS = 8
N_ADA = 9

LANES = 128
SUBLANES = 8
VMEM_LIMIT_BYTES = 60 * 1024 * 1024

SSD_CHUNK = 128
HEAD_DIM = 64
CAST_BLOCK_BYTES = 8 * 1024 * 1024


def _params(semantics):
    return pltpu.CompilerParams(dimension_semantics=semantics,
                                vmem_limit_bytes=VMEM_LIMIT_BYTES)


def _pick(total, want):
    t = min(total, want)
    while total % t:
        t //= 2
    return t


def _silu(v):
    return v * jax.nn.sigmoid(v)


def _softplus(v):
    return jnp.maximum(v, 0.0) + jnp.log1p(jnp.exp(-jnp.abs(v)))


def _rms_mod(x, nw, shift, scale):
    y = x * lax.rsqrt(jnp.mean(x * x, axis=-1, keepdims=True) + EPS) * nw
    return y * (1.0 + scale) + shift


def _dot(a, b):
    return jnp.dot(a, b, preferred_element_type=F32)


def _cast_body(w_ref, o_ref):
    tc = o_ref.shape[-1]
    for t in range(o_ref.shape[0]):
        o_ref[t] = w_ref[:, t * tc:(t + 1) * tc].astype(BF16)


def _to_bf16(w, tc):
    depth, rows, cols = w.shape
    want = 1 << int(math.log2(max(SUBLANES, CAST_BLOCK_BYTES // (4 * cols))))
    tr = _pick(rows, want)
    nt = cols // tc
    per_layer = rows // tr
    return pl.pallas_call(
        _cast_body,
        grid=(depth * per_layer,),
        in_specs=[pl.BlockSpec((None, tr, cols), lambda i: (i // per_layer, i % per_layer, 0))],
        out_specs=pl.BlockSpec((None, nt, tr, tc), lambda i: (i // per_layer, 0, i % per_layer, 0)),
        out_shape=jax.ShapeDtypeStruct((depth, nt, rows, tc), BF16),
        compiler_params=_params(("parallel",)),
        name="cast",
    )(w)


def _repack_body(w_ref, o_ref, odt_ref, *, p_dt, heads, conf_c, gate_w, half):
    p_glu = p_dt + heads
    p_gate = p_glu + 2 * conf_c

    def put(dst, src, width):
        for c in range(0, width, LANES):
            o_ref[:, dst + c:dst + c + LANES] = w_ref[src + c:src + c + LANES, :].T.astype(BF16)

    put(0, 0, p_dt)
    put(p_dt, p_gate, gate_w)
    for i in range(conf_c // half):
        dst = p_dt + gate_w + 2 * i * half
        put(dst, p_glu + i * half, half)
        put(dst + half, p_glu + conf_c + i * half, half)
    lane = lax.broadcasted_iota(jnp.int32, odt_ref.shape, 1)
    odt_ref[...] = jnp.where(lane < heads, w_ref[p_dt:p_dt + LANES, :].T, 0.0).astype(BF16)


def _repack_w_in(w_in, p_dt, heads, conf_c, gate_w, half):
    depth, d, cols = w_in.shape
    tr = LANES
    assert d % tr == 0
    per_layer = d // tr
    n_main = cols - heads
    w_in = jnp.swapaxes(w_in, 1, 2)
    return pl.pallas_call(
        functools.partial(_repack_body, p_dt=p_dt, heads=heads, conf_c=conf_c, gate_w=gate_w, half=half),
        grid=(depth * per_layer,),
        in_specs=[pl.BlockSpec((None, cols, tr), lambda i: (i // per_layer, 0, i % per_layer))],
        out_specs=[pl.BlockSpec((None, tr, n_main), lambda i: (i // per_layer, i % per_layer, 0)),
                   pl.BlockSpec((None, tr, LANES), lambda i: (i // per_layer, i % per_layer, 0))],
        out_shape=[jax.ShapeDtypeStruct((depth, d, n_main), BF16),
                   jax.ShapeDtypeStruct((depth, d, LANES), BF16)],
        compiler_params=_params(("parallel",)),
        name="repack",
    )(w_in)


def _ada_body(c_ref, w_ref, b_ref, o_ref):
    c = c_ref[...]
    o_ref[0] = _dot(_silu(c).astype(BF16), w_ref[0].astype(BF16)) + b_ref[0]


def _ada(c, w_ada, b_ada):
    depth, d, n = w_ada.shape
    bsz = c.shape[0]
    rows = -(-bsz // SUBLANES) * SUBLANES
    c_pad = jnp.pad(c, ((0, rows - bsz), (0, 0)))
    tn = _pick(n, 1024)
    out = pl.pallas_call(
        _ada_body,
        grid=(depth, n // tn),
        in_specs=[
            pl.BlockSpec((rows, d), lambda l, j: (0, 0)),
            pl.BlockSpec((1, d, tn), lambda l, j: (l, 0, j)),
            pl.BlockSpec((1, 1, tn), lambda l, j: (l, 0, j)),
        ],
        out_specs=pl.BlockSpec((1, rows, tn), lambda l, j: (l, 0, j)),
        out_shape=jax.ShapeDtypeStruct((depth, rows, n), F32),
        compiler_params=_params(("parallel", "parallel")),
        name="ada",
    )(c_pad, w_ada, b_ada.reshape(depth, 1, n))
    return out[:, :bsz].reshape(depth, bsz, N_ADA, d)


def _ffn_body(x_ref, ada_ref, nw_ref, w1_ref, w3_ref, w2_ref, fw_ref, o_ref, h_ref,
              *, ada_base, final):
    j = pl.program_id(1)

    @pl.when(j == 0)
    def _():
        shift = ada_ref[0, ada_base:ada_base + 1, :]
        scale = ada_ref[0, ada_base + 1:ada_base + 2, :]
        h_ref[...] = _rms_mod(x_ref[...], nw_ref[...], shift, scale).astype(BF16)
        o_ref[...] = jnp.zeros(o_ref.shape, F32)

    h = h_ref[...]
    a = _dot(h, w1_ref[...])
    g = _dot(h, w3_ref[...])
    u = (_silu(g) * a).astype(BF16)
    o_ref[...] += _dot(u, w2_ref[...])

    @pl.when(j == pl.num_programs(1) - 1)
    def _():
        gate = ada_ref[0, ada_base + 2:ada_base + 3, :]
        y = x_ref[...] + 0.5 * gate * o_ref[...]
        if final:
            y = y * lax.rsqrt(jnp.mean(y * y, axis=-1, keepdims=True) + EPS) * fw_ref[...]
        o_ref[...] = y


def _ffn(x2, ada, ada_base, nw, w13, w2, layer, fw, seq, final):
    m, d = x2.shape
    tf = w13.shape[3]
    nf = w13.shape[1] // 2
    assert w2.shape[2] == nf * tf and w2.shape[3] == d
    tm = _pick(seq, 1024)
    per_batch = seq // tm
    return pl.pallas_call(
        functools.partial(_ffn_body, ada_base=ada_base, final=final),
        grid=(m // tm, nf),
        in_specs=[
            pl.BlockSpec((tm, d), lambda i, j: (i, 0), pipeline_mode=pl.Buffered(1)),
            pl.BlockSpec((1, N_ADA, d), lambda i, j: (i // per_batch, 0, 0)),
            pl.BlockSpec((1, d), lambda i, j: (0, 0)),
            pl.BlockSpec((None, None, d, tf), lambda i, j: (layer, j, 0, 0)),
            pl.BlockSpec((None, None, d, tf), lambda i, j: (layer, j + nf, 0, 0)),
            pl.BlockSpec((None, None, tf, d), lambda i, j: (layer, 0, j, 0)),
            pl.BlockSpec((1, d), lambda i, j: (0, 0)),
        ],
        out_specs=pl.BlockSpec((tm, d), lambda i, j: (i, 0)),
        out_shape=jax.ShapeDtypeStruct((m, d), F32),
        scratch_shapes=[pltpu.VMEM((tm, d), BF16)],
        compiler_params=_params(("parallel", "arbitrary")),
        name="ffn",
    )(x2, ada, nw.reshape(1, d), w13, w13, w2, fw.reshape(1, d))


def _halo_rows(taps):
    return -(-(taps - 1) // SUBLANES) * SUBLANES


def _shift_residues(taps):
    base = _halo_rows(taps) - (taps - 1)
    return sorted({(base + k) % SUBLANES for k in range(taps)} - {0})


def _conv_lane_block(ext_ref, sh_ref, w_ref, b_ref, cols, taps, rows, rb, emit):
    halo = _halo_rows(taps)
    base = halo - (taps - 1)
    span = halo + rows - SUBLANES
    slot = {s: i for i, s in enumerate(_shift_residues(taps))}
    for s, i in slot.items():
        sh_ref[i, 0:span, :] = ext_ref[s:s + span, cols]
    for r0 in range(0, rows, rb):
        acc = None
        for k in range(taps):
            s = (base + k) % SUBLANES
            a = base + k - s + r0
            win = ext_ref[a:a + rb, cols] if s == 0 else sh_ref[slot[s], a:a + rb, :]
            term = w_ref[k:k + 1, cols] * win
            acc = term if acc is None else acc + term
        emit(r0, acc + b_ref[:, cols])


def _inproj_body(x_ref, ada_ref, nw_ref, w_ref, wdt_ref, dtb_ref, cw_ref, cb_ref,
                 zx_ref, gate_ref, u_ref, dt_ref, h_ref, ext_ref, sh_ref, carry_ref,
                 *, ada_base, n_z, n_x, n_gate, n_glu, per_batch, taps, rb, lb):
    i = pl.program_id(0)
    j = pl.program_id(1)
    n_zx = n_z + n_x
    n_main = n_zx + n_gate + n_glu
    tm, tn = zx_ref.shape
    halo = _halo_rows(taps)

    @pl.when(j == 0)
    def _():
        shift = ada_ref[0, ada_base:ada_base + 1, :]
        scale = ada_ref[0, ada_base + 1:ada_base + 2, :]
        h_ref[...] = _rms_mod(x_ref[...], nw_ref[...], shift, scale).astype(BF16)

    @pl.when(j < n_z)
    def _():
        zx_ref[...] = _dot(h_ref[...], w_ref[...])

    @pl.when(jnp.logical_and(j >= n_z, j < n_zx))
    def _():
        xt = j - n_z

        @pl.when(i % per_batch == 0)
        def _():
            ext_ref[0:halo, :] = jnp.zeros((halo, tn), F32)

        @pl.when(i % per_batch != 0)
        def _():
            ext_ref[0:halo, :] = carry_ref[xt]

        ext_ref[halo:halo + tm, :] = _dot(h_ref[...], w_ref[...])
        carry_ref[xt] = ext_ref[tm:tm + halo, :]
        for c0 in range(0, tn, lb):
            cols = slice(c0, c0 + lb)

            def emit(r0, acc, cols=cols):
                zx_ref[r0:r0 + rb, cols] = _silu(acc)

            _conv_lane_block(ext_ref, sh_ref, cw_ref, cb_ref, cols, taps, tm, rb, emit)

    @pl.when(jnp.logical_and(j >= n_zx, j < n_zx + n_gate))
    def _():
        gate_ref[...] = jax.nn.sigmoid(_dot(h_ref[...], w_ref[...]))

    @pl.when(jnp.logical_and(j >= n_zx + n_gate, j < n_main))
    def _():
        acc = _dot(h_ref[...], w_ref[...])
        half = acc.shape[1] // 2
        u_ref[...] = acc[:, :half] * jax.nn.sigmoid(acc[:, half:])

    @pl.when(j == n_main)
    def _():
        dt_ref[...] = _softplus(_dot(h_ref[...], wdt_ref[...]) + dtb_ref[...])


def _inproj(x2, ada, ada_base, nw, w_main, w_dt, layer, dt_b, conv_w, conv_b, widths, seq, tn):
    m, d = x2.shape
    z_w, gate_w, glu_w = widths
    taps, x_w = conv_w.shape
    n_z, n_x, n_gate, n_glu = z_w // tn, x_w // tn, gate_w // tn, 2 * glu_w // tn
    n_zx = n_z + n_x
    n_main = n_zx + n_gate + n_glu
    tm = _pick(seq, 1024)
    per_batch = seq // tm
    dtw = w_dt.shape[2]
    halo = _halo_rows(taps)
    lb = _pick(tn, 256)
    body = functools.partial(_inproj_body, ada_base=ada_base, n_z=n_z, n_x=n_x, n_gate=n_gate,
                             n_glu=n_glu, per_batch=per_batch, taps=taps, rb=_pick(tm, 64), lb=lb)
    conv_tile = lambda i, j: (0, jnp.clip(j - n_z, 0, n_x - 1))
    return pl.pallas_call(
        body,
        grid=(m // tm, n_main + 1),
        in_specs=[
            pl.BlockSpec((tm, d), lambda i, j: (i, 0), pipeline_mode=pl.Buffered(1)),
            pl.BlockSpec((1, N_ADA, d), lambda i, j: (i // per_batch, 0, 0)),
            pl.BlockSpec((1, d), lambda i, j: (0, 0)),
            pl.BlockSpec((None, d, tn), lambda i, j: (layer, 0, jnp.minimum(j, n_main - 1))),
            pl.BlockSpec((None, d, dtw), lambda i, j: (layer, 0, 0)),
            pl.BlockSpec((1, dtw), lambda i, j: (0, 0)),
            pl.BlockSpec((taps, tn), conv_tile),
            pl.BlockSpec((1, tn), conv_tile),
        ],
        out_specs=[
            pl.BlockSpec((tm, tn), lambda i, j: (i, jnp.minimum(j, n_zx - 1))),
            pl.BlockSpec((tm, tn), lambda i, j: (i, jnp.clip(j - n_zx, 0, n_gate - 1))),
            pl.BlockSpec((tm, tn // 2), lambda i, j: (i, jnp.clip(j - n_zx - n_gate, 0, n_glu - 1))),
            pl.BlockSpec((tm, dtw), lambda i, j: (i, 0)),
        ],
        out_shape=[
            jax.ShapeDtypeStruct((m, z_w + x_w), F32),
            jax.ShapeDtypeStruct((m, gate_w), F32),
            jax.ShapeDtypeStruct((m, glu_w), F32),
            jax.ShapeDtypeStruct((m, dtw), F32),
        ],
        scratch_shapes=[pltpu.VMEM((tm, d), BF16),
                        pltpu.VMEM((halo + tm, tn), F32),
                        pltpu.VMEM((len(_shift_residues(taps)), halo + tm, lb), F32),
                        pltpu.VMEM((n_x, halo, tn), F32)],
        compiler_params=_params(("arbitrary", "arbitrary")),
        name="inproj",
    )(x2, ada, nw.reshape(1, d), w_main, w_dt, dt_b, conv_w, conv_b.reshape(1, x_w))


def _conf_conv_body(u_ref, w_ref, b_ref, g_ref, beta_ref, o_ref, ext_ref, sh_ref, y_ref,
                    *, taps, rb, cb):
    rows, width = u_ref.shape[1], u_ref.shape[2]
    halo = _halo_rows(taps)
    first = pl.program_id(1) == 0

    @pl.when(first)
    def _():
        ext_ref[0:halo, :] = jnp.zeros((halo, width), F32)

    @pl.when(jnp.logical_not(first))
    def _():
        ext_ref[0:halo, :] = ext_ref[rows:rows + halo, :]

    ext_ref[halo:halo + rows, :] = u_ref[0]

    def lane_block(ci, carry):
        cols = pl.ds(pl.multiple_of(ci * cb, cb), cb)

        def emit(r0, acc):
            y_ref[r0:r0 + rb, cols] = acc

        _conv_lane_block(ext_ref, sh_ref, w_ref, b_ref, cols, taps, rows, rb, emit)
        return carry

    lax.fori_loop(0, width // cb, lane_block, 0)
    y = y_ref[...]
    mu = jnp.mean(y, axis=-1, keepdims=True)
    yc = y - mu
    var = jnp.mean(yc * yc, axis=-1, keepdims=True)
    yn = yc * lax.rsqrt(var + EPS) * g_ref[...] + beta_ref[...]
    o_ref[0] = _silu(yn).astype(BF16)


def _conf_conv(u3, dw_w, dw_b, ln_g, ln_b):
    bsz, seq, c = u3.shape
    taps = dw_w.shape[0]
    tr = _pick(seq, 256)
    cb = _pick(c, LANES)
    halo = _halo_rows(taps)
    vec = lambda v: v.reshape(1, c)
    row_spec = pl.BlockSpec((1, c), lambda b, t: (0, 0))
    return pl.pallas_call(
        functools.partial(_conf_conv_body, taps=taps, rb=_pick(tr, 128), cb=cb),
        grid=(bsz, seq // tr),
        in_specs=[
            pl.BlockSpec((1, tr, c), lambda b, t: (b, t, 0)),
            pl.BlockSpec((taps, c), lambda b, t: (0, 0)),
            row_spec, row_spec, row_spec,
        ],
        out_specs=pl.BlockSpec((1, tr, c), lambda b, t: (b, t, 0)),
        out_shape=jax.ShapeDtypeStruct((bsz, seq, c), BF16),
        scratch_shapes=[pltpu.VMEM((tr + halo, c), F32),
                        pltpu.VMEM((len(_shift_residues(taps)), tr + halo, cb), F32),
                        pltpu.VMEM((tr, c), F32)],
        compiler_params=_params(("parallel", "arbitrary")),
        name="conf_conv",
    )(u3, dw_w, vec(dw_b), vec(ln_g), vec(ln_b))


def _split3(v):
    hi = v.astype(BF16)
    r1 = v - hi.astype(F32)
    mid = r1.astype(BF16)
    lo = (r1 - mid.astype(F32)).astype(BF16)
    return hi, mid, lo


def _pack_split(v, lo_half):
    hi = v.astype(BF16).astype(F32)
    r1 = v - hi
    mid = r1.astype(BF16).astype(F32)
    lo = r1 - mid
    first = jnp.where(lo_half, hi, pltpu.roll(mid, HEAD_DIM, 1))
    return jnp.concatenate([first.astype(BF16), lo.astype(BF16)], axis=1)


def _ssd_body(xs_ref, b_ref, c_ref, z_ref, dt_ref, alog_ref, dsk_ref, nw_ref, e_ref,
              yg_ref, ss_ref, s_ref, acsp_ref, dtp_ref, acst_ref, *, heads_per_group):
    t = pl.program_id(1)
    g = pl.program_id(2)
    q = SSD_CHUNK
    rows = xs_ref.shape[1]
    n_chunks = rows // q
    n_pairs = xs_ref.shape[2] // LANES
    causal = lax.broadcasted_iota(jnp.int32, (q, q), 0) >= lax.broadcasted_iota(jnp.int32, (q, q), 1)
    lane = lax.broadcasted_iota(jnp.int32, (q, LANES), 1)
    lo_half = lane < HEAD_DIM

    @pl.when(t == 0)
    def _():
        s_ref[g] = jnp.zeros(s_ref.shape[1:], F32)

    @pl.when(g == 0)
    def _():
        ss_ref[...] = jnp.zeros(ss_ref.shape, F32)
        tri = causal.astype(BF16)
        a2 = -jnp.exp(alog_ref[...]) * LOG2E
        for ci in range(n_chunks):
            rs = slice(ci * q, (ci + 1) * q)
            hi, mid, lo = _split3(dt_ref[0, rs, :] * a2)
            cs = _dot(tri, jnp.concatenate([hi, mid, lo], axis=1))
            acs = (cs[:, :LANES] + cs[:, LANES:2 * LANES]) + cs[:, 2 * LANES:]
            acsp_ref[ci] = _pack_split(acs, lo_half)
            dtp_ref[ci] = _pack_split(dt_ref[0, rs, :], lo_half)
            acst_ref[ci] = acs.T

    head_row0 = pl.multiple_of(g * heads_per_group, SUBLANES)
    n_bc = heads_per_group * LANES

    pre = []
    for ci in range(n_chunks):
        rs = slice(ci * q, (ci + 1) * q)
        spread = _dot(acsp_ref[ci], e_ref[...])
        dt_pairs = _dot(dtp_ref[ci], e_ref[:, n_bc:])
        acs_t = acst_ref[ci, pl.ds(head_row0, SUBLANES), :]
        bg = b_ref[0, rs, :].astype(BF16)
        cg = c_ref[0, rs, :].astype(BF16)
        cb = lax.dot_general(cg, bg, (((1,), (1,)), ((), ())), preferred_element_type=F32)
        y_diag, off_scale, xw_parts, decay_parts = [], [], [], []
        for p in range(n_pairs):
            ls = slice(p * LANES, (p + 1) * LANES)
            h0, h1 = 2 * p, 2 * p + 1
            bc0 = spread[:, h0 * LANES:(h0 + 1) * LANES]
            bc1 = spread[:, h1 * LANES:(h1 + 1) * LANES]
            m0 = cb * jnp.exp2(jnp.where(causal, bc0 - acs_t[h0:h0 + 1, :], -jnp.inf))
            m1 = cb * jnp.exp2(jnp.where(causal, bc1 - acs_t[h1:h1 + 1, :], -jnp.inf))
            acs_pair = spread[:, n_bc + p * LANES:n_bc + (p + 1) * LANES]
            xdt = xs_ref[0, rs, ls] * dt_pairs[:, ls]
            lhs = jnp.concatenate([m0.astype(BF16), m1.astype(BF16)], axis=1)
            rhs = jnp.concatenate([jnp.where(lo_half, xdt, 0.0).astype(BF16),
                                   jnp.where(lo_half, 0.0, xdt).astype(BF16)], axis=0)
            y_diag.append(_dot(lhs, rhs))
            off_scale.append(jnp.exp2(acs_pair))
            last_pair = acs_pair[q - 1:q, :]
            xw_parts.append((xdt * jnp.exp2(last_pair - acs_pair)).astype(BF16))
            decay_parts.append(jnp.exp2(last_pair))
        xw = jnp.concatenate(xw_parts, axis=1)
        new = lax.dot_general(bg, xw, (((0,), (0,)), ((), ())), preferred_element_type=F32)
        pre.append((cg, y_diag, off_scale, jnp.concatenate(decay_parts, axis=1), new))

    state = s_ref[g]
    for ci in range(n_chunks):
        rs = slice(ci * q, (ci + 1) * q)
        cg, y_diag, off_scale, decay, new = pre[ci]
        y_off = _dot(cg, state.astype(BF16))
        state = state * decay + new
        sq = jnp.zeros((q, LANES), F32)
        for p in range(n_pairs):
            ls = slice(p * LANES, (p + 1) * LANES)
            y = y_diag[p] + y_off[:, ls] * off_scale[p] + dsk_ref[0, :, ls] * xs_ref[0, rs, ls]
            v = y * _silu(z_ref[0, rs, ls])
            sq = sq + v * v
            yg_ref[0, rs, ls] = (v * nw_ref[:, ls]).astype(BF16)
        ss_ref[0, rs, :] += sq
    s_ref[g] = state


def _head_spread_matrices(groups, hpg):
    r = jnp.arange(2 * LANES)
    head_of_row = jnp.where(r < 3 * HEAD_DIM, r % HEAD_DIM, -1)
    n = jnp.arange(hpg * LANES + hpg * HEAD_DIM)
    local_head = jnp.where(n < hpg * LANES, n // LANES, (n - hpg * LANES) // HEAD_DIM)
    g = jnp.arange(groups)
    want = g[:, None, None] * hpg + local_head[None, None, :]
    return (head_of_row[None, :, None] == want).astype(BF16)


def _ssd(zx3, dt3, a_pad, dsk, nw, d_inner):
    bsz, seq, width = zx3.shape
    groups = SSM_GROUPS
    gw = d_inner // groups
    n_state = (width - 2 * d_inner) // (2 * groups)
    hpg = gw // HEAD_DIM
    assert gw % LANES == 0 and d_inner % n_state == 0 and hpg == SUBLANES
    assert dt3.shape[2] == LANES and groups * hpg <= HEAD_DIM
    spread = _head_spread_matrices(groups, hpg)
    tr = _pick(seq, 512)
    assert tr % SSD_CHUNK == 0
    x_off = d_inner // gw
    b_off = 2 * d_inner // n_state
    return pl.pallas_call(
        functools.partial(_ssd_body, heads_per_group=hpg),
        grid=(bsz, seq // tr, groups),
        in_specs=[
            pl.BlockSpec((1, tr, gw), lambda b, t, g: (b, t, x_off + g)),
            pl.BlockSpec((1, tr, n_state), lambda b, t, g: (b, t, b_off + g)),
            pl.BlockSpec((1, tr, n_state), lambda b, t, g: (b, t, b_off + groups + g)),
            pl.BlockSpec((1, tr, gw), lambda b, t, g: (b, t, g)),
            pl.BlockSpec((1, tr, LANES), lambda b, t, g: (b, t, 0)),
            pl.BlockSpec((1, LANES), lambda b, t, g: (0, 0)),
            pl.BlockSpec((1, 1, gw), lambda b, t, g: (g, 0, 0)),
            pl.BlockSpec((1, gw), lambda b, t, g: (0, g)),
            pl.BlockSpec((None,) + spread.shape[1:], lambda b, t, g: (g, 0, 0)),
        ],
        out_specs=[
            pl.BlockSpec((1, tr, gw), lambda b, t, g: (b, t, g)),
            pl.BlockSpec((1, tr, LANES), lambda b, t, g: (b, t, 0)),
        ],
        out_shape=[
            jax.ShapeDtypeStruct((bsz, seq, d_inner), BF16),
            jax.ShapeDtypeStruct((bsz, seq, LANES), F32),
        ],
        scratch_shapes=[pltpu.VMEM((groups, n_state, gw), F32),
                        pltpu.VMEM((tr // SSD_CHUNK, SSD_CHUNK, 2 * LANES), BF16),
                        pltpu.VMEM((tr // SSD_CHUNK, SSD_CHUNK, 2 * LANES), BF16),
                        pltpu.VMEM((tr // SSD_CHUNK, LANES, SSD_CHUNK), F32)],
        compiler_params=_params(("parallel", "arbitrary", "arbitrary")),
        name="ssd",
    )(zx3, zx3, zx3, zx3, dt3, a_pad, dsk, nw.reshape(1, d_inner), spread)


def _merge_body(yg_ref, ss_ref, uc_ref, gs_ref, gc_ref, x_ref, ada_ref, wso_ref, wpw_ref,
                bpw_ref, wo_ref, o_ref, *, ada_base, d_inner):
    j = pl.program_id(1)

    @pl.when(j == 0)
    def _():
        o_ref[...] = jnp.zeros(o_ref.shape, F32)

    mean_sq = jnp.sum(ss_ref[...], axis=-1, keepdims=True) * (1.0 / d_inner)
    y_ssm = _dot(yg_ref[...], wso_ref[...]) * lax.rsqrt(mean_sq + EPS)
    y_conv = _dot(uc_ref[...], wpw_ref[...]) + bpw_ref[...]
    merged = (gs_ref[...] * y_ssm + gc_ref[...] * y_conv).astype(BF16)
    o_ref[...] += _dot(merged, wo_ref[...])

    @pl.when(j == pl.num_programs(1) - 1)
    def _():
        o_ref[...] = x_ref[...] + ada_ref[0, ada_base + 2:ada_base + 3, :] * o_ref[...]


def _merge(yg2, ss2, uc2, gates2, x2, ada, ada_base, wso, wpw, bpw, wo, layer, seq):
    m, d = x2.shape
    d_inner = yg2.shape[1]
    c = uc2.shape[1]
    tm = _pick(seq, 512)
    tn = wso.shape[3]
    nn = d // tn
    per_batch = seq // tm
    return pl.pallas_call(
        functools.partial(_merge_body, ada_base=ada_base, d_inner=d_inner),
        grid=(m // tm, nn),
        in_specs=[
            pl.BlockSpec((tm, d_inner), lambda i, j: (i, 0)),
            pl.BlockSpec((tm, LANES), lambda i, j: (i, 0)),
            pl.BlockSpec((tm, c), lambda i, j: (i, 0)),
            pl.BlockSpec((tm, tn), lambda i, j: (i, j)),
            pl.BlockSpec((tm, tn), lambda i, j: (i, j + nn)),
            pl.BlockSpec((tm, d), lambda i, j: (i, 0)),
            pl.BlockSpec((1, N_ADA, d), lambda i, j: (i // per_batch, 0, 0)),
            pl.BlockSpec((None, None, d_inner, tn), lambda i, j: (layer, j, 0, 0)),
            pl.BlockSpec((None, None, c, tn), lambda i, j: (layer, j, 0, 0)),
            pl.BlockSpec((1, tn), lambda i, j: (0, j)),
            pl.BlockSpec((None, None, tn, d), lambda i, j: (layer, 0, j, 0)),
        ],
        out_specs=pl.BlockSpec((tm, d), lambda i, j: (i, 0)),
        out_shape=jax.ShapeDtypeStruct((m, d), F32),
        compiler_params=_params(("parallel", "arbitrary")),
        name="merge",
    )(yg2, ss2, uc2, gates2, gates2, x2, ada, wso, wpw, bpw.reshape(1, d), wo)


def _pad_lanes(v):
    return jnp.pad(v, [(0, 0)] * (v.ndim - 1) + [(0, LANES - v.shape[-1])])


def kernel(x, c, w_ada, b_ada, norm_ffn1, ffn1_w13, ffn1_w2, norm_mix, w_in, ssm_conv_w, ssm_conv_b, dt_bias, a_log, d_skip, ssm_norm_w, w_ssm_out, dw_w, dw_b, conv_ln_g, conv_ln_b, w_pw2, b_pw2, w_o, norm_ffn2, ffn2_w13, ffn2_w2, final_norm):
    bsz, seq, d = x.shape
    depth = w_ada.shape[0]
    groups = SSM_GROUPS
    d_inner = w_ssm_out.shape[1]
    conv_dim = ssm_conv_w.shape[2]
    heads = dt_bias.shape[1]
    conf_c = dw_w.shape[2]
    assert d_inner // heads == HEAD_DIM
    p_dt = d_inner + conv_dim
    p_glu = p_dt + heads
    p_gate = p_glu + 2 * conf_c
    gate_w = w_in.shape[2] - p_gate
    tn = _pick(math.gcd(d_inner, conv_dim, gate_w, 2 * conf_c), 1024)

    ada_all = _ada(c, w_ada, b_ada)
    tf = _pick(ffn1_w2.shape[1], 512)
    w13_1, w2_1 = _to_bf16(ffn1_w13, tf), _to_bf16(ffn1_w2, d)
    w13_2, w2_2 = _to_bf16(ffn2_w13, tf), _to_bf16(ffn2_w2, d)
    tmn = _pick(d, 512)
    wso, wpw, wo = _to_bf16(w_ssm_out, tmn), _to_bf16(w_pw2, tmn), _to_bf16(w_o, d)
    w_main, w_dt = _repack_w_in(w_in, p_dt, heads, conf_c, gate_w, tn // 2)
    x2 = x.reshape(bsz * seq, d)
    for l in range(depth):
        ada = ada_all[l]
        dt_b = _pad_lanes(dt_bias[l]).reshape(1, LANES)
        a_pad = _pad_lanes(a_log[l]).reshape(1, LANES)
        dsk = jnp.repeat(d_skip[l], HEAD_DIM).reshape(groups, 1, d_inner // groups)

        x2 = _ffn(x2, ada, 0, norm_ffn1[l], w13_1, w2_1, l, final_norm, seq, False)

        zx, gates, u, dt = _inproj(x2, ada, 3, norm_mix[l], w_main, w_dt, l, dt_b,
                                   ssm_conv_w[l], ssm_conv_b[l], (d_inner, gate_w, conf_c), seq, tn)

        yg, ss = _ssd(zx.reshape(bsz, seq, p_dt), dt.reshape(bsz, seq, LANES), a_pad, dsk,
                      ssm_norm_w[l], d_inner)
        uc = _conf_conv(u.reshape(bsz, seq, conf_c), dw_w[l], dw_b[l], conv_ln_g[l], conv_ln_b[l])

        x2 = _merge(yg.reshape(bsz * seq, d_inner), ss.reshape(bsz * seq, LANES),
                    uc.reshape(bsz * seq, conf_c), gates, x2, ada, 3,
                    wso, wpw, b_pw2[l], wo, l, seq)

        x2 = _ffn(x2, ada, 6, norm_ffn2[l], w13_2, w2_2, l, final_norm, seq, l == depth - 1)
    return x2.reshape(bsz, seq, d)
```
